```python
import math
import jax, jax.numpy as jnp
from jax import lax
import numpy as np

D_MODEL = 4096
BATCH = 4
SEQ = 4096
DEPTH = 1

MIX_WIDTH = D_MODEL
ATTN_WIDTH = MIX_WIDTH // 2
POOL_WIDTH = MIX_WIDTH - ATTN_WIDTH
DIFF_HEAD_DIM = 128
N_DIFF_HEADS = ATTN_WIDTH // (2 * DIFF_HEAD_DIM)
V_HEAD_DIM = 2 * DIFF_HEAD_DIM
ROPE_DIM = DIFF_HEAD_DIM // 4
ROPE_THETA = 500000.0
Q_BLOCK = 128
SUBLN_EPS = 1e-5
POOL_WINDOWS = (2, 4, 8, 16)
N_POOL_GROUPS = len(POOL_WINDOWS)
POOL_GROUP_DIM = POOL_WIDTH // N_POOL_GROUPS
MAX_WINDOW = max(POOL_WINDOWS)
IN_WIDTH = 3 * ATTN_WIDTH + POOL_WIDTH
D_FF = ((8 * D_MODEL // 3 + 255) // 256) * 256
CONV_WIDTH = 3
NORM_EPS = 1e-5
POS_OFFSET_MAX = 1024

kernel_name = 'hymba_diffattn_pool_convffn_block'


def rms_norm(x, g, eps):
    x32 = x.astype(jnp.float32)
    y = x32 * lax.rsqrt(jnp.mean(x32 * x32, axis=-1, keepdims=True) + eps)
    return (y * g.astype(jnp.float32)).astype(x.dtype)


def apply_partial_rope(t, cos, sin):
    half = ROPE_DIM // 2
    t32 = t[..., :ROPE_DIM].astype(jnp.float32)
    t1, t2 = t32[..., :half], t32[..., half:]
    rot = jnp.concatenate([t1 * cos - t2 * sin, t2 * cos + t1 * sin], axis=-1).astype(t.dtype)
    return jnp.concatenate([rot, t[..., ROPE_DIM:]], axis=-1)


def diff_attention(q, k, v, lam, subln_g, lambda_init):
    B, S = q.shape[0], q.shape[1]
    q = q * (DIFF_HEAD_DIM ** -0.5)
    outs = []
    for blk in range(S // Q_BLOCK):
        q0 = blk * Q_BLOCK
        kv_len = q0 + Q_BLOCK
        qb = q[:, q0:kv_len]
        kb = k[:, :kv_len]
        vb = v[:, :kv_len]
        s = jnp.einsum('bqhcd,bkhcd->bhcqk', qb, kb).astype(jnp.float32)
        q_idx = q0 + jnp.arange(Q_BLOCK)
        k_idx = jnp.arange(kv_len)
        mask = k_idx[None, :] <= q_idx[:, None]
        s = jnp.where(mask, s, -1e30)
        p = jax.nn.softmax(s, axis=-1)
        a = (p[:, :, 0] - lam * p[:, :, 1]).astype(v.dtype)
        outs.append(jnp.einsum('bhqk,bkhe->bqhe', a, vb))
    o = jnp.concatenate(outs, axis=1)
    o = rms_norm(o, subln_g, SUBLN_EPS) * (1.0 - lambda_init)
    return o.reshape(B, S, ATTN_WIDTH)


def multiscale_pool(u, w_pool, pool_scale):
    B, S, _ = u.shape
    u32 = u.astype(jnp.float32).reshape(B, S, N_POOL_GROUPS, POOL_GROUP_DIM)
    c = jnp.cumsum(u32, axis=1)
    c_pad = jnp.pad(c, ((0, 0), (MAX_WINDOW, 0), (0, 0), (0, 0)))
    counts_base = jnp.arange(1, S + 1)
    groups = []
    for g, w in enumerate(POOL_WINDOWS):
        lagged = c_pad[:, MAX_WINDOW - w:MAX_WINDOW - w + S, g]
        count = jnp.minimum(counts_base, w).astype(jnp.float32)
        mean = (c[:, :, g] - lagged) / count[None, :, None]
        groups.append(mean - u32[:, :, g])
    pooled = jnp.stack(groups, axis=2).astype(u.dtype)
    y = jnp.einsum('bsgc,gcd->bsgd', pooled, w_pool).reshape(B, S, POOL_WIDTH)
    return y * pool_scale


def setup_inputs(seed: int = 0) -> dict:
    key = jax.random.key(seed)
    ks = jax.random.split(key, 20)
    f32 = jnp.float32
    x = jax.random.normal(ks[0], (BATCH, SEQ, D_MODEL), f32)
    offsets = jax.random.randint(ks[1], (BATCH, 1), 0, POS_OFFSET_MAX, dtype=jnp.int32)
    positions = (offsets + jnp.arange(SEQ, dtype=jnp.int32)[None, :]).astype(jnp.int32)
    norm1_g = 1.0 + 0.02 * jax.random.normal(ks[2], (DEPTH, D_MODEL), f32)
    w_in = jax.random.normal(ks[3], (DEPTH, D_MODEL, IN_WIDTH), f32) * D_MODEL ** -0.5
    lambda_q1 = 0.1 * jax.random.normal(ks[4], (DEPTH, DIFF_HEAD_DIM), f32)
    lambda_k1 = 0.1 * jax.random.normal(ks[5], (DEPTH, DIFF_HEAD_DIM), f32)
    lambda_q2 = 0.1 * jax.random.normal(ks[6], (DEPTH, DIFF_HEAD_DIM), f32)
    lambda_k2 = 0.1 * jax.random.normal(ks[7], (DEPTH, DIFF_HEAD_DIM), f32)
    subln_g = 1.0 + 0.02 * jax.random.normal(ks[8], (DEPTH, V_HEAD_DIM), f32)
    w_pool = jax.random.normal(ks[9], (DEPTH, N_POOL_GROUPS, POOL_GROUP_DIM, POOL_GROUP_DIM), f32) * POOL_GROUP_DIM ** -0.5
    pool_scale = 1.0 + 0.02 * jax.random.normal(ks[10], (DEPTH, POOL_WIDTH), f32)
    w_out = jax.random.normal(ks[11], (DEPTH, MIX_WIDTH, D_MODEL), f32) * MIX_WIDTH ** -0.5
    norm2_g = 1.0 + 0.02 * jax.random.normal(ks[12], (DEPTH, D_MODEL), f32)
    w_gate = jax.random.normal(ks[13], (DEPTH, D_MODEL, D_FF), f32) * D_MODEL ** -0.5
    w_up = jax.random.normal(ks[14], (DEPTH, D_MODEL, D_FF), f32) * D_MODEL ** -0.5
    conv_w = jax.random.normal(ks[15], (DEPTH, CONV_WIDTH, D_FF), f32) * CONV_WIDTH ** -0.5
    conv_b = 0.01 * jax.random.normal(ks[16], (DEPTH, D_FF), f32)
    w_down = jax.random.normal(ks[17], (DEPTH, D_FF, D_MODEL), f32) * D_FF ** -0.5
    norm_f_g = 1.0 + 0.02 * jax.random.normal(ks[18], (D_MODEL,), f32)
    return {'x': x, 'positions': positions, 'norm1_g': norm1_g, 'w_in': w_in,
            'lambda_q1': lambda_q1, 'lambda_k1': lambda_k1, 'lambda_q2': lambda_q2, 'lambda_k2': lambda_k2,
            'subln_g': subln_g, 'w_pool': w_pool, 'pool_scale': pool_scale, 'w_out': w_out,
            'norm2_g': norm2_g, 'w_gate': w_gate, 'w_up': w_up, 'conv_w': conv_w, 'conv_b': conv_b,
            'w_down': w_down, 'norm_f_g': norm_f_g}


def reference(x, positions, norm1_g, w_in, lambda_q1, lambda_k1, lambda_q2, lambda_k2, subln_g,
              w_pool, pool_scale, w_out, norm2_g, w_gate, w_up, conv_w, conv_b, w_down, norm_f_g):
    B, S, _ = x.shape
    inv_freq = 1.0 / (ROPE_THETA ** (jnp.arange(0, ROPE_DIM, 2, dtype=jnp.float32) / ROPE_DIM))
    angles = positions.astype(jnp.float32)[..., None] * inv_freq
    cos = jnp.cos(angles)[:, :, None, None, :]
    sin = jnp.sin(angles)[:, :, None, None, :]
    h = x
    for l in range(DEPTH):
        lambda_init = 0.8 - 0.6 * math.exp(-0.3 * l)
        n = rms_norm(h, norm1_g[l], NORM_EPS)
        proj = n @ w_in[l]
        q, k, v, u = jnp.split(proj, [ATTN_WIDTH, 2 * ATTN_WIDTH, 3 * ATTN_WIDTH], axis=-1)
        q = apply_partial_rope(q.reshape(B, S, N_DIFF_HEADS, 2, DIFF_HEAD_DIM), cos, sin)
        k = apply_partial_rope(k.reshape(B, S, N_DIFF_HEADS, 2, DIFF_HEAD_DIM), cos, sin)
        v = v.reshape(B, S, N_DIFF_HEADS, V_HEAD_DIM)
        lam = (jnp.exp(jnp.sum(lambda_q1[l].astype(jnp.float32) * lambda_k1[l].astype(jnp.float32)))
               - jnp.exp(jnp.sum(lambda_q2[l].astype(jnp.float32) * lambda_k2[l].astype(jnp.float32)))
               + lambda_init)
        attn_out = diff_attention(q, k, v, lam, subln_g[l], lambda_init)
        pool_out = multiscale_pool(u, w_pool[l], pool_scale[l])
        h = h + jnp.concatenate([attn_out, pool_out], axis=-1) @ w_out[l]
        n2 = rms_norm(h, norm2_g[l], NORM_EPS)
        gate = n2 @ w_gate[l]
        up = n2 @ w_up[l]
        gp = jnp.pad(gate, ((0, 0), (CONV_WIDTH - 1, 0), (0, 0)))
        cw = conv_w[l]
        conv = conv_b[l] + cw[0] * gp[:, 0:S]
        for j in range(1, CONV_WIDTH):
            conv = conv + cw[j] * gp[:, j:j + S]
        h = h + (jax.nn.silu(conv) * up) @ w_down[l]
    return rms_norm(h, norm_f_g, NORM_EPS)
```

```python
import functools
import math

import jax
import jax.numpy as jnp
from jax import lax
from jax.experimental import pallas as pl
from jax.experimental.pallas import tpu as pltpu

F32 = jnp.float32
BF16 = jnp.bfloat16

DIFF_HEAD_DIM = 128
V_HEAD_DIM = 2 * DIFF_HEAD_DIM
ROPE_DIM = DIFF_HEAD_DIM // 4
ROPE_HALF = ROPE_DIM // 2
ROPE_THETA = 500000.0
SUBLN_EPS = 1e-5
NORM_EPS = 1e-5
POOL_WINDOWS = (2, 4, 8, 16)
MAX_WINDOW = max(POOL_WINDOWS)
CONV_WIDTH = 3
MASK_VALUE = -1e30

LANES = 128
BF16_SUBLANES = 16
VMEM_LIMIT_BYTES = 58 * 1024 * 1024
HALO = BF16_SUBLANES

assert HALO >= MAX_WINDOW and HALO >= CONV_WIDTH - 1


def _tile(dim, pref):
    t = min(dim, pref)
    while dim % t:
        t -= 1
    return t


def _rms(x, g, eps):
    return x * lax.rsqrt(jnp.mean(x * x, axis=-1, keepdims=True) + eps) * g


def _params(*sem):
    return pltpu.CompilerParams(dimension_semantics=sem, vmem_limit_bytes=VMEM_LIMIT_BYTES)


def _inproj_kernel(x_ref, g_ref, w_ref, c_ref, sa_ref, sb_ref, qkv_ref, u_ref, n_ref, *, a_tiles, scale):
    j = pl.program_id(1)

    @pl.when(j == 0)
    def _():
        n_ref[...] = _rms(x_ref[...], g_ref[...], NORM_EPS).astype(BF16)

    acc = jnp.dot(n_ref[...], w_ref[...], preferred_element_type=F32)
    n_comp = acc.shape[1] // LANES

    def rope_store(mult):
        c, sa, sb = c_ref[...], sa_ref[...], sb_ref[...]
        for k in range(n_comp):
            t = acc[:, k * LANES:(k + 1) * LANES]
            r = (t * c + pltpu.roll(t, LANES - ROPE_HALF, 1) * sa + pltpu.roll(t, ROPE_HALF, 1) * sb)
            if mult is not None:
                r = r * mult
            qkv_ref[:, k * LANES:(k + 1) * LANES] = r.astype(BF16)

    @pl.when(j < a_tiles)
    def _():
        rope_store(scale)

    @pl.when((j >= a_tiles) & (j < 2 * a_tiles))
    def _():
        rope_store(None)

    @pl.when((j >= 2 * a_tiles) & (j < 3 * a_tiles))
    def _():
        qkv_ref[...] = acc.astype(BF16)

    @pl.when(j >= 3 * a_tiles)
    def _():
        u_ref[...] = acc


def _inproj(h, g, w, rope_c, rope_sa, rope_sb, attn_width):
    t, d = h.shape
    n_out = w.shape[1]
    pool_width = n_out - 3 * attn_width
    tm = _tile(t, 512)
    tn = _tile(math.gcd(attn_width, pool_width), 1024)
    a_tiles = attn_width // tn
    qkv_last = 3 * a_tiles - 1
    return pl.pallas_call(
        functools.partial(_inproj_kernel, a_tiles=a_tiles, scale=DIFF_HEAD_DIM ** -0.5),
        grid=(t // tm, n_out // tn),
        in_specs=[
            pl.BlockSpec((tm, d), lambda i, j: (i, 0)),
            pl.BlockSpec((1, d), lambda i, j: (0, 0)),
            pl.BlockSpec((d, tn), lambda i, j: (0, j)),
            pl.BlockSpec((tm, LANES), lambda i, j: (i, 0)),
            pl.BlockSpec((tm, LANES), lambda i, j: (i, 0)),
            pl.BlockSpec((tm, LANES), lambda i, j: (i, 0)),
        ],
        out_specs=[
            pl.BlockSpec((tm, tn), lambda i, j: (i, jnp.minimum(j, qkv_last))),
            pl.BlockSpec((tm, tn), lambda i, j: (i, jnp.maximum(j - (qkv_last + 1), 0))),
        ],
        out_shape=[
            jax.ShapeDtypeStruct((t, 3 * attn_width), BF16),
            jax.ShapeDtypeStruct((t, pool_width), F32),
        ],
        scratch_shapes=[pltpu.VMEM((tm, d), BF16)],
        compiler_params=_params("parallel", "arbitrary"),
        name="inproj",
    )(h, g, w, rope_c, rope_sa, rope_sb)


def _attn_kernel(lq1_ref, lk1_ref, lq2_ref, lk2_ref, sg_ref, q_ref, k_ref, v_ref, o_ref, *, tq, lambda_init):
    qi = pl.program_id(2)
    dh = DIFF_HEAD_DIM
    q = (q_ref[:, :dh], q_ref[:, dh:])

    def step(kv0, carry, masked):
        v = v_ref[pl.ds(kv0, tq), :]
        out = []
        for c in range(2):
            m, l, acc = carry[c]
            k = k_ref[pl.ds(kv0, tq), c * dh:(c + 1) * dh]
            s = lax.dot_general(q[c], k, (((1,), (1,)), ((), ())), preferred_element_type=F32)
            if masked:
                row = lax.broadcasted_iota(jnp.int32, s.shape, 0)
                col = lax.broadcasted_iota(jnp.int32, s.shape, 1)
                s = jnp.where(col <= row, s, MASK_VALUE)
            m_new = jnp.maximum(m, jnp.max(s, axis=-1, keepdims=True))
            alpha = jnp.exp(m - m_new)
            p = jnp.exp(s - m_new)
            l = alpha * l + jnp.sum(p, axis=-1, keepdims=True)
            acc = alpha * acc + jnp.dot(p.astype(BF16), v, preferred_element_type=F32)
            out.append((m_new, l, acc))
        return tuple(out)

    init = tuple((jnp.full((tq, 1), MASK_VALUE, F32), jnp.zeros((tq, 1), F32),
                  jnp.zeros((tq, V_HEAD_DIM), F32)) for _ in range(2))
    carry = lax.fori_loop(0, qi, lambda j, c: step(pl.multiple_of(j * tq, tq), c, False), init)
    (_, l0, a0), (_, l1, a1) = step(pl.multiple_of(qi * tq, tq), carry, True)

    lam = (jnp.exp(jnp.sum(lq1_ref[...] * lk1_ref[...], axis=-1, keepdims=True))
           - jnp.exp(jnp.sum(lq2_ref[...] * lk2_ref[...], axis=-1, keepdims=True)) + lambda_init)
    o = a0 * (1.0 / l0) - lam * (a1 * (1.0 / l1))
    o_ref[...] = (_rms(o, sg_ref[...], SUBLN_EPS) * (1.0 - lambda_init)).astype(BF16)


def _attention(qkv, lq1, lk1, lq2, lk2, subln_g, batch, seq, attn_width, lambda_init):
    t = qkv.shape[0]
    heads = attn_width // V_HEAD_DIM
    tq = _tile(seq, 512)
    nq = seq // tq
    vec = pl.BlockSpec((1, DIFF_HEAD_DIM), lambda b, h, i: (0, 0))
    return pl.pallas_call(
        functools.partial(_attn_kernel, tq=tq, lambda_init=lambda_init),
        grid=(batch, heads, nq),
        in_specs=[
            vec, vec, vec, vec,
            pl.BlockSpec((1, V_HEAD_DIM), lambda b, h, i: (0, 0)),
            pl.BlockSpec((tq, V_HEAD_DIM), lambda b, h, i: (b * nq + i, h)),
            pl.BlockSpec((seq, V_HEAD_DIM), lambda b, h, i: (b, heads + h)),
            pl.BlockSpec((seq, V_HEAD_DIM), lambda b, h, i: (b, 2 * heads + h)),
        ],
        out_specs=pl.BlockSpec((tq, V_HEAD_DIM), lambda b, h, i: (b * nq + i, h)),
        out_shape=jax.ShapeDtypeStruct((t, attn_width), BF16),
        compiler_params=_params("parallel", "parallel", "arbitrary"),
        name="diff_attention",
    )(lq1, lk1, lq2, lk2, subln_g, qkv, qkv, qkv)


def _pool_kernel(u_ref, w_ref, sc_ref, o_ref, *, seq, rows):
    g = pl.program_id(0)
    cols = u_ref.shape[1]

    def chunk(r0, ext, window):
        s, span = ext, 1
        while span < window:
            s = s + pltpu.roll(s, span, 0)
            span *= 2
        u = ext[HALO:]
        pos = r0 + lax.broadcasted_iota(jnp.int32, (rows, 1), 0)
        inv_count = 1.0 / jnp.minimum(pos + 1, window).astype(F32)
        pooled = s[HALO:] * inv_count - u
        y = jnp.dot(pooled.astype(BF16), w_ref[...], preferred_element_type=F32) * sc_ref[...]
        o_ref[pl.ds(r0, rows), :] = y.astype(BF16)

    def run(window):
        chunk(0, jnp.concatenate([jnp.zeros((HALO, cols), F32), u_ref[pl.ds(0, rows), :]], axis=0), window)

        def body(c, carry):
            r0 = pl.multiple_of(c * rows, rows)
            chunk(r0, u_ref[pl.ds(r0 - HALO, rows + HALO), :], window)
            return carry

        lax.fori_loop(1, seq // rows, body, 0)

    for gi, window in enumerate(POOL_WINDOWS):
        pl.when(g == gi)(functools.partial(run, window))


def _pooling(u, w_pool, pool_scale, batch, seq):
    t, width = u.shape
    groups = len(POOL_WINDOWS)
    cols = width // groups
    rows = _tile(seq, 512)
    return pl.pallas_call(
        functools.partial(_pool_kernel, seq=seq, rows=rows),
        grid=(groups, batch),
        in_specs=[
            pl.BlockSpec((seq, cols), lambda g, b: (b, g)),
            pl.BlockSpec((None, cols, cols), lambda g, b: (g, 0, 0)),
            pl.BlockSpec((1, cols), lambda g, b: (0, g)),
        ],
        out_specs=pl.BlockSpec((seq, cols), lambda g, b: (b, g)),
        out_shape=jax.ShapeDtypeStruct((t, width), BF16),
        compiler_params=_params("parallel", "parallel"),
        name="pool",
    )(u, w_pool, pool_scale)


def _outproj_kernel(a_ref, p_ref, wa_ref, wp_ref, x_ref, h_ref):
    acc = jnp.dot(a_ref[...], wa_ref[...], preferred_element_type=F32)
    acc = acc + jnp.dot(p_ref[...], wp_ref[...], preferred_element_type=F32)
    h_ref[...] = x_ref[...] + acc


def _outproj(attn, pool, w_out, x):
    t, d = x.shape
    aw, pw = attn.shape[1], pool.shape[1]
    assert aw == pw, "the two mixer groups are equally wide"
    tm = _tile(t, 512)
    tn = _tile(d, 1024)
    return pl.pallas_call(
        _outproj_kernel,
        grid=(t // tm, d // tn),
        in_specs=[
            pl.BlockSpec((tm, aw), lambda i, j: (i, 0)),
            pl.BlockSpec((tm, pw), lambda i, j: (i, 0)),
            pl.BlockSpec((aw, tn), lambda i, j: (0, j)),
            pl.BlockSpec((pw, tn), lambda i, j: (1, j)),
            pl.BlockSpec((tm, tn), lambda i, j: (i, j)),
        ],
        out_specs=pl.BlockSpec((tm, tn), lambda i, j: (i, j)),
        out_shape=jax.ShapeDtypeStruct((t, d), F32),
        compiler_params=_params("parallel", "parallel"),
        name="outproj",
    )(attn, pool, w_out, w_out, x)


def _mlp_kernel(h_ref, halo_ref, g2_ref, wg_ref, wu_ref, cw_ref, cb_ref, wd_ref, gf_ref, o_ref, n_ref,
                *, tm, tiles_per_seq, final_norm):
    i = pl.program_id(0)
    j = pl.program_id(1)

    @pl.when(j == 0)
    def _():
        keep = (i % tiles_per_seq != 0).astype(F32)
        n_ref[:HALO, :] = (_rms(halo_ref[...], g2_ref[...], NORM_EPS) * keep).astype(BF16)
        h = h_ref[...]
        n_ref[HALO:, :] = _rms(h, g2_ref[...], NORM_EPS).astype(BF16)
        o_ref[...] = h

    gate = jnp.dot(n_ref[...], wg_ref[...], preferred_element_type=F32)
    up = jnp.dot(n_ref[HALO:, :], wu_ref[...], preferred_element_type=F32)
    conv = cb_ref[...]
    for tap in range(CONV_WIDTH):
        start = HALO - (CONV_WIDTH - 1) + tap
        conv = conv + cw_ref[tap:tap + 1, :] * gate[start:start + tm]
    act = jax.nn.silu(conv) * up
    o_ref[...] += jnp.dot(act.astype(BF16), wd_ref[...], preferred_element_type=F32)

    if final_norm:
        @pl.when(j == pl.num_programs(1) - 1)
        def _():
            o_ref[...] = _rms(o_ref[...], gf_ref[...], NORM_EPS)


def _mlp(h, g2, w_gate, w_up, conv_w, conv_b, w_down, gf, seq, final_norm):
    t, d = h.shape
    f = w_gate.shape[1]
    tm = _tile(seq, 512)
    tf = _tile(f, 256)
    assert tm % HALO == 0
    halo_blocks = tm // HALO
    return pl.pallas_call(
        functools.partial(_mlp_kernel, tm=tm, tiles_per_seq=seq // tm, final_norm=final_norm),
        grid=(t // tm, f // tf),
        in_specs=[
            pl.BlockSpec((tm, d), lambda i, j: (i, 0)),
            pl.BlockSpec((HALO, d), lambda i, j: (jnp.maximum(i * halo_blocks - 1, 0), 0)),
            pl.BlockSpec((1, d), lambda i, j: (0, 0)),
            pl.BlockSpec((d, tf), lambda i, j: (0, j)),
            pl.BlockSpec((d, tf), lambda i, j: (0, j)),
            pl.BlockSpec((CONV_WIDTH, tf), lambda i, j: (0, j)),
            pl.BlockSpec((1, tf), lambda i, j: (0, j)),
            pl.BlockSpec((tf, d), lambda i, j: (j, 0)),
            pl.BlockSpec((1, d), lambda i, j: (0, 0)),
        ],
        out_specs=pl.BlockSpec((tm, d), lambda i, j: (i, 0)),
        out_shape=jax.ShapeDtypeStruct((t, d), F32),
        scratch_shapes=[pltpu.VMEM((HALO + tm, d), BF16)],
        compiler_params=_params("parallel", "arbitrary"),
        name="conv_mlp",
    )(h, h, g2, w_gate, w_up, conv_w, conv_b, w_down, gf)


def _rope_tables(positions):
    inv_freq = 1.0 / (ROPE_THETA ** (jnp.arange(0, ROPE_DIM, 2, dtype=F32) / ROPE_DIM))
    angles = positions.astype(F32).reshape(-1, 1) * inv_freq
    cos, sin = jnp.cos(angles), jnp.sin(angles)
    t = angles.shape[0]
    zeros = functools.partial(jnp.zeros, dtype=F32)
    c = jnp.concatenate([cos, cos, jnp.ones((t, LANES - ROPE_DIM), F32)], axis=1)
    sa = jnp.concatenate([-sin, zeros((t, LANES - ROPE_HALF))], axis=1)
    sb = jnp.concatenate([zeros((t, ROPE_HALF)), sin, zeros((t, LANES - ROPE_DIM))], axis=1)
    return c, sa, sb


def kernel(x, positions, norm1_g, w_in, lambda_q1, lambda_k1, lambda_q2, lambda_k2, subln_g, w_pool, pool_scale, w_out, norm2_g, w_gate, w_up, conv_w, conv_b, w_down, norm_f_g):
    batch, seq, d = x.shape
    depth = w_in.shape[0]
    pool_width = pool_scale.shape[1]
    attn_width = (w_in.shape[2] - pool_width) // 3
    rope_c, rope_sa, rope_sb = _rope_tables(positions)
    row = lambda v: v.reshape(1, -1)
    h = x.reshape(batch * seq, d)
    for l in range(depth):
        lambda_init = 0.8 - 0.6 * math.exp(-0.3 * l)
        qkv, u = _inproj(h, row(norm1_g[l]), w_in[l].astype(BF16), rope_c, rope_sa, rope_sb, attn_width)
        attn = _attention(qkv, row(lambda_q1[l]), row(lambda_k1[l]), row(lambda_q2[l]), row(lambda_k2[l]),
                          row(subln_g[l]), batch, seq, attn_width, lambda_init)
        pool = _pooling(u, w_pool[l].astype(BF16), row(pool_scale[l]), batch, seq)
        h = _outproj(attn, pool, w_out[l].astype(BF16), h)
        h = _mlp(h, row(norm2_g[l]), w_gate[l].astype(BF16), w_up[l].astype(BF16), conv_w[l], row(conv_b[l]),
                 w_down[l].astype(BF16), row(norm_f_g), seq, final_norm=(l == depth - 1))
    return h.reshape(batch, seq, d)
```

```python
import functools
import math

import jax
import jax.numpy as jnp
from jax import lax
from jax.experimental import pallas as pl
from jax.experimental.pallas import tpu as pltpu

F32 = jnp.float32
BF16 = jnp.bfloat16

DIFF_HEAD_DIM = 128
V_HEAD_DIM = 2 * DIFF_HEAD_DIM
ROPE_DIM = DIFF_HEAD_DIM // 4
ROPE_HALF = ROPE_DIM // 2
ROPE_THETA = 500000.0
SUBLN_EPS = 1e-5
NORM_EPS = 1e-5
POOL_WINDOWS = (2, 4, 8, 16)
MAX_WINDOW = max(POOL_WINDOWS)
CONV_WIDTH = 3
MASK_VALUE = -1e30

LANES = 128
MXU_WIDTH = 256
NORM_ROWS = 64
DOWN_SLAB = MXU_WIDTH
BF16_SUBLANES = 16
VMEM_LIMIT_BYTES = 58 * 1024 * 1024
HALO = BF16_SUBLANES

assert HALO >= MAX_WINDOW and HALO >= CONV_WIDTH - 1


def _tile(dim, pref):
    t = min(dim, pref)
    while dim % t:
        t -= 1
    return t


def _rms(x, g, eps):
    return x * lax.rsqrt(jnp.mean(x * x, axis=-1, keepdims=True) + eps) * g


def _rms_rows(src_ref, g_ref, dst_ref, dst_row0=0, copy_ref=None):
    def body(c, carry):
        r0 = pl.multiple_of(c * NORM_ROWS, NORM_ROWS)
        x = src_ref[pl.ds(r0, NORM_ROWS), :]
        if copy_ref is not None:
            copy_ref[pl.ds(r0, NORM_ROWS), :] = x
        dst_ref[pl.ds(dst_row0 + r0, NORM_ROWS), :] = _rms(x, g_ref[...], NORM_EPS).astype(dst_ref.dtype)
        return carry

    lax.fori_loop(0, src_ref.shape[0] // NORM_ROWS, body, 0)


def _params(*sem):
    return pltpu.CompilerParams(dimension_semantics=sem, vmem_limit_bytes=VMEM_LIMIT_BYTES)


def _inproj_kernel(x_ref, g_ref, w_ref, c_ref, sa_ref, sb_ref, qkv_ref, u_ref, n_ref, *, a_tiles, scale):
    j = pl.program_id(1)

    @pl.when(j == 0)
    def _():
        _rms_rows(x_ref, g_ref, n_ref)

    tn = w_ref.shape[1]
    sub = min(tn, MXU_WIDTH)

    def rope(t, mult):
        r = (t * c_ref[...] + pltpu.roll(t, LANES - ROPE_HALF, 1) * sa_ref[...]
             + pltpu.roll(t, ROPE_HALF, 1) * sb_ref[...])
        return r if mult is None else r * mult

    def project(store):
        for c0 in range(0, tn, sub):
            store(c0, jnp.dot(n_ref[...], w_ref[:, c0:c0 + sub], preferred_element_type=F32))

    def store_rope(mult, c0, acc):
        for k in range(0, sub, LANES):
            qkv_ref[:, c0 + k:c0 + k + LANES] = rope(acc[:, k:k + LANES], mult).astype(BF16)

    def store_v(c0, acc):
        qkv_ref[:, c0:c0 + sub] = acc.astype(BF16)

    def store_u(c0, acc):
        u_ref[:, c0:c0 + sub] = acc

    pl.when(j < a_tiles)(lambda: project(functools.partial(store_rope, scale)))
    pl.when((j >= a_tiles) & (j < 2 * a_tiles))(lambda: project(functools.partial(store_rope, None)))
    pl.when((j >= 2 * a_tiles) & (j < 3 * a_tiles))(lambda: project(store_v))
    pl.when(j >= 3 * a_tiles)(lambda: project(store_u))


def _inproj(h, g, w, rope_c, rope_sa, rope_sb, attn_width):
    t, d = h.shape
    n_out = w.shape[1]
    pool_width = n_out - 3 * attn_width
    tm = _tile(t, 512)
    tn = _tile(math.gcd(attn_width, pool_width), 1024)
    a_tiles = attn_width // tn
    qkv_last = 3 * a_tiles - 1
    return pl.pallas_call(
        functools.partial(_inproj_kernel, a_tiles=a_tiles, scale=DIFF_HEAD_DIM ** -0.5),
        grid=(t // tm, n_out // tn),
        in_specs=[
            pl.BlockSpec((tm, d), lambda i, j: (i, 0)),
            pl.BlockSpec((1, d), lambda i, j: (0, 0)),
            pl.BlockSpec((d, tn), lambda i, j: (0, j)),
            pl.BlockSpec((tm, LANES), lambda i, j: (i, 0)),
            pl.BlockSpec((tm, LANES), lambda i, j: (i, 0)),
            pl.BlockSpec((tm, LANES), lambda i, j: (i, 0)),
        ],
        out_specs=[
            pl.BlockSpec((tm, tn), lambda i, j: (i, jnp.minimum(j, qkv_last))),
            pl.BlockSpec((tm, tn), lambda i, j: (i, jnp.maximum(j - (qkv_last + 1), 0))),
        ],
        out_shape=[
            jax.ShapeDtypeStruct((t, 3 * attn_width), BF16),
            jax.ShapeDtypeStruct((t, pool_width), F32),
        ],
        scratch_shapes=[pltpu.VMEM((tm, d), BF16)],
        compiler_params=_params("parallel", "arbitrary"),
        name="inproj",
    )(h, g, w, rope_c, rope_sa, rope_sb)


def _attn_kernel(lq1_ref, lk1_ref, lq2_ref, lk2_ref, sg_ref, q_ref, k_ref, v_ref, o_ref,
                 vt_ref, acc_ref, sa_ref, sb_ref, m_ref, l_ref, *, tq, lambda_init):
    qi = pl.program_id(2)
    dh = DIFF_HEAD_DIM
    n_kv = vt_ref.shape[0]

    @pl.when(qi == 0)
    def _():
        for j in range(n_kv):
            vt_ref[j] = v_ref[j * tq:(j + 1) * tq, :].T

    acc_ref[...] = jnp.zeros_like(acc_ref)
    for c in range(2):
        m_ref[c] = jnp.full((1, tq), MASK_VALUE, F32)
        l_ref[c] = jnp.zeros((1, tq), F32)

    def scores(j, dst_ref):
        kv0 = pl.multiple_of(j * tq, tq)
        for c in range(2):
            dst_ref[c] = lax.dot_general(
                k_ref[pl.ds(kv0, tq), c * dh:(c + 1) * dh], q_ref[:, c * dh:(c + 1) * dh],
                (((1,), (1,)), ((), ())), preferred_element_type=F32)

    def softmax_pv(j, src_ref, masked):
        vt = vt_ref[j]
        for c in range(2):
            st = src_ref[c]
            if masked:
                key = lax.broadcasted_iota(jnp.int32, st.shape, 0)
                qry = lax.broadcasted_iota(jnp.int32, st.shape, 1)
                st = jnp.where(key <= qry, st, MASK_VALUE)
            m_prev = m_ref[c]
            m_new = jnp.maximum(m_prev, jnp.max(st, axis=0, keepdims=True))
            alpha = jnp.exp(m_prev - m_new)
            pt = jnp.exp(st - m_new)
            m_ref[c] = m_new
            l_ref[c] = alpha * l_ref[c] + jnp.sum(pt, axis=0, keepdims=True)
            acc_ref[c] = alpha * acc_ref[c] + jnp.dot(vt, pt.astype(BF16), preferred_element_type=F32)

    def step(j, src_ref, dst_ref):
        softmax_pv(j, src_ref, False)
        scores(j + 1, dst_ref)

    def pair(jj, carry):
        step(2 * jj, sa_ref, sb_ref)
        step(2 * jj + 1, sb_ref, sa_ref)
        return carry

    scores(0, sa_ref)
    lax.fori_loop(0, qi // 2, pair, 0)

    @pl.when(qi % 2 == 1)
    def _():
        step(qi - 1, sa_ref, sb_ref)
        softmax_pv(qi, sb_ref, True)

    @pl.when(qi % 2 == 0)
    def _():
        softmax_pv(qi, sa_ref, True)

    lam = (jnp.exp(jnp.sum(lq1_ref[...] * lk1_ref[...], axis=-1, keepdims=True))
           - jnp.exp(jnp.sum(lq2_ref[...] * lk2_ref[...], axis=-1, keepdims=True)) + lambda_init)
    ot = acc_ref[0] * (1.0 / l_ref[0]) - lam * (acc_ref[1] * (1.0 / l_ref[1]))
    yt = ot * lax.rsqrt(jnp.mean(ot * ot, axis=0, keepdims=True) + SUBLN_EPS) * sg_ref[...]
    o_ref[...] = (yt * (1.0 - lambda_init)).T.astype(BF16)


def _attention(qkv, lq1, lk1, lq2, lk2, subln_g, batch, seq, attn_width, lambda_init):
    t = qkv.shape[0]
    heads = attn_width // V_HEAD_DIM
    tq = _tile(seq, 512)
    nq = seq // tq
    vec = pl.BlockSpec((1, DIFF_HEAD_DIM), lambda b, h, i: (0, 0))
    return pl.pallas_call(
        functools.partial(_attn_kernel, tq=tq, lambda_init=lambda_init),
        grid=(batch, heads, nq),
        in_specs=[
            vec, vec, vec, vec,
            pl.BlockSpec((V_HEAD_DIM, 1), lambda b, h, i: (0, 0)),
            pl.BlockSpec((tq, V_HEAD_DIM), lambda b, h, i: (b * nq + i, h)),
            pl.BlockSpec((seq, V_HEAD_DIM), lambda b, h, i: (b, heads + h)),
            pl.BlockSpec((seq, V_HEAD_DIM), lambda b, h, i: (b, 2 * heads + h)),
        ],
        out_specs=pl.BlockSpec((tq, V_HEAD_DIM), lambda b, h, i: (b * nq + i, h)),
        out_shape=jax.ShapeDtypeStruct((t, attn_width), BF16),
        scratch_shapes=[
            pltpu.VMEM((nq, V_HEAD_DIM, tq), BF16),
            pltpu.VMEM((2, V_HEAD_DIM, tq), F32),
            pltpu.VMEM((2, tq, tq), F32),
            pltpu.VMEM((2, tq, tq), F32),
            pltpu.VMEM((2, 1, tq), F32),
            pltpu.VMEM((2, 1, tq), F32),
        ],
        compiler_params=_params("parallel", "parallel", "arbitrary"),
        name="diff_attention",
    )(lq1, lk1, lq2, lk2, subln_g.reshape(V_HEAD_DIM, 1), qkv, qkv, qkv)


def _pool_kernel(u_ref, w_ref, sc_ref, o_ref, *, seq, rows):
    g = pl.program_id(0)
    cols = u_ref.shape[1]

    def chunk(r0, ext, window):
        s, span = ext, 1
        while span < window:
            s = s + pltpu.roll(s, span, 0)
            span *= 2
        u = ext[HALO:]
        pos = r0 + lax.broadcasted_iota(jnp.int32, (rows, 1), 0)
        inv_count = 1.0 / jnp.minimum(pos + 1, window).astype(F32)
        pooled = s[HALO:] * inv_count - u
        y = jnp.dot(pooled.astype(BF16), w_ref[...], preferred_element_type=F32) * sc_ref[...]
        o_ref[pl.ds(r0, rows), :] = y.astype(BF16)

    def run(window):
        chunk(0, jnp.concatenate([jnp.zeros((HALO, cols), F32), u_ref[pl.ds(0, rows), :]], axis=0), window)

        def body(c, carry):
            r0 = pl.multiple_of(c * rows, rows)
            chunk(r0, u_ref[pl.ds(r0 - HALO, rows + HALO), :], window)
            return carry

        lax.fori_loop(1, seq // rows, body, 0)

    for gi, window in enumerate(POOL_WINDOWS):
        pl.when(g == gi)(functools.partial(run, window))


def _pooling(u, w_pool, pool_scale, batch, seq):
    t, width = u.shape
    groups = len(POOL_WINDOWS)
    cols = width // groups
    rows = _tile(seq, 512)
    return pl.pallas_call(
        functools.partial(_pool_kernel, seq=seq, rows=rows),
        grid=(groups, batch),
        in_specs=[
            pl.BlockSpec((seq, cols), lambda g, b: (b, g)),
            pl.BlockSpec((None, cols, cols), lambda g, b: (g, 0, 0)),
            pl.BlockSpec((1, cols), lambda g, b: (0, g)),
        ],
        out_specs=pl.BlockSpec((seq, cols), lambda g, b: (b, g)),
        out_shape=jax.ShapeDtypeStruct((t, width), BF16),
        compiler_params=_params("parallel", "parallel"),
        name="pool",
    )(u, w_pool, pool_scale)


def _outproj_kernel(a_ref, p_ref, wa_ref, wp_ref, x_ref, h_ref):
    acc = jnp.dot(a_ref[...], wa_ref[...], preferred_element_type=F32)
    acc = acc + jnp.dot(p_ref[...], wp_ref[...], preferred_element_type=F32)
    h_ref[...] = x_ref[...] + acc


def _outproj(attn, pool, w_out, x):
    t, d = x.shape
    aw, pw = attn.shape[1], pool.shape[1]
    assert aw == pw, "the two mixer groups are equally wide"
    tm = _tile(t, 512)
    tn = _tile(d, 1024)
    return pl.pallas_call(
        _outproj_kernel,
        grid=(t // tm, d // tn),
        in_specs=[
            pl.BlockSpec((tm, aw), lambda i, j: (i, 0)),
            pl.BlockSpec((tm, pw), lambda i, j: (i, 0)),
            pl.BlockSpec((aw, tn), lambda i, j: (0, j)),
            pl.BlockSpec((pw, tn), lambda i, j: (1, j)),
            pl.BlockSpec((tm, tn), lambda i, j: (i, j)),
        ],
        out_specs=pl.BlockSpec((tm, tn), lambda i, j: (i, j)),
        out_shape=jax.ShapeDtypeStruct((t, d), F32),
        compiler_params=_params("parallel", "parallel"),
        name="outproj",
    )(attn, pool, w_out, w_out, x)


def _mlp_kernel(h_ref, halo_ref, g2_ref, wg_ref, wu_ref, cw_ref, cb_ref, wd_ref, gf_ref, o_ref, n_ref,
                *, tm, tiles_per_seq, final_norm):
    i = pl.program_id(0)
    j = pl.program_id(1)

    @pl.when(j == 0)
    def _():
        keep = (i % tiles_per_seq != 0).astype(F32)
        n_ref[:HALO, :] = (_rms(halo_ref[...], g2_ref[...], NORM_EPS) * keep).astype(BF16)
        _rms_rows(h_ref, g2_ref, n_ref, dst_row0=HALO, copy_ref=o_ref)

    gate = jnp.dot(n_ref[...], wg_ref[...], preferred_element_type=F32)
    up = jnp.dot(n_ref[HALO:, :], wu_ref[...], preferred_element_type=F32)
    conv = cb_ref[...]
    for tap in range(CONV_WIDTH):
        start = HALO - (CONV_WIDTH - 1) + tap
        conv = conv + cw_ref[tap:tap + 1, :] * gate[start:start + tm]
    act = (jax.nn.silu(conv) * up).astype(BF16)
    for c0 in range(0, o_ref.shape[1], DOWN_SLAB):
        cols = slice(c0, c0 + DOWN_SLAB)
        o_ref[:, cols] += jnp.dot(act, wd_ref[:, cols], preferred_element_type=F32)

    if final_norm:
        @pl.when(j == pl.num_programs(1) - 1)
        def _():
            _rms_rows(o_ref, gf_ref, o_ref)


def _mlp(h, g2, w_gate, w_up, conv_w, conv_b, w_down, gf, seq, final_norm):
    t, d = h.shape
    f = w_gate.shape[1]
    tm = _tile(seq, 512)
    tf = _tile(f, 256)
    assert tm % HALO == 0
    halo_blocks = tm // HALO
    return pl.pallas_call(
        functools.partial(_mlp_kernel, tm=tm, tiles_per_seq=seq // tm, final_norm=final_norm),
        grid=(t // tm, f // tf),
        in_specs=[
            pl.BlockSpec((tm, d), lambda i, j: (i, 0)),
            pl.BlockSpec((HALO, d), lambda i, j: (jnp.maximum(i * halo_blocks - 1, 0), 0)),
            pl.BlockSpec((1, d), lambda i, j: (0, 0)),
            pl.BlockSpec((d, tf), lambda i, j: (0, j)),
            pl.BlockSpec((d, tf), lambda i, j: (0, j)),
            pl.BlockSpec((CONV_WIDTH, tf), lambda i, j: (0, j)),
            pl.BlockSpec((1, tf), lambda i, j: (0, j)),
            pl.BlockSpec((tf, d), lambda i, j: (j, 0)),
            pl.BlockSpec((1, d), lambda i, j: (0, 0)),
        ],
        out_specs=pl.BlockSpec((tm, d), lambda i, j: (i, 0)),
        out_shape=jax.ShapeDtypeStruct((t, d), F32),
        scratch_shapes=[pltpu.VMEM((HALO + tm, d), BF16)],
        compiler_params=_params("parallel", "arbitrary"),
        name="conv_mlp",
    )(h, h, g2, w_gate, w_up, conv_w, conv_b, w_down, gf)


def _rope_tables(positions):
    inv_freq = 1.0 / (ROPE_THETA ** (jnp.arange(0, ROPE_DIM, 2, dtype=F32) / ROPE_DIM))
    angles = positions.astype(F32).reshape(-1, 1) * inv_freq
    cos, sin = jnp.cos(angles), jnp.sin(angles)
    t = angles.shape[0]
    zeros = functools.partial(jnp.zeros, dtype=F32)
    c = jnp.concatenate([cos, cos, jnp.ones((t, LANES - ROPE_DIM), F32)], axis=1)
    sa = jnp.concatenate([-sin, zeros((t, LANES - ROPE_HALF))], axis=1)
    sb = jnp.concatenate([zeros((t, ROPE_HALF)), sin, zeros((t, LANES - ROPE_DIM))], axis=1)
    return c, sa, sb


def kernel(x, positions, norm1_g, w_in, lambda_q1, lambda_k1, lambda_q2, lambda_k2, subln_g, w_pool, pool_scale, w_out, norm2_g, w_gate, w_up, conv_w, conv_b, w_down, norm_f_g):
    batch, seq, d = x.shape
    depth = w_in.shape[0]
    pool_width = pool_scale.shape[1]
    attn_width = (w_in.shape[2] - pool_width) // 3
    rope_c, rope_sa, rope_sb = _rope_tables(positions)
    row = lambda v: v.reshape(1, -1)
    h = x.reshape(batch * seq, d)
    for l in range(depth):
        lambda_init = 0.8 - 0.6 * math.exp(-0.3 * l)
        qkv, u = _inproj(h, row(norm1_g[l]), w_in[l].astype(BF16), rope_c, rope_sa, rope_sb, attn_width)
        attn = _attention(qkv, row(lambda_q1[l]), row(lambda_k1[l]), row(lambda_q2[l]), row(lambda_k2[l]),
                          row(subln_g[l]), batch, seq, attn_width, lambda_init)
        pool = _pooling(u, w_pool[l].astype(BF16), row(pool_scale[l]), batch, seq)
        h = _outproj(attn, pool, w_out[l].astype(BF16), h)
        h = _mlp(h, row(norm2_g[l]), w_gate[l].astype(BF16), w_up[l].astype(BF16), conv_w[l], row(conv_b[l]),
                 w_down[l].astype(BF16), row(norm_f_g), seq, final_norm=(l == depth - 1))
    return h.reshape(batch, seq, d)
```

```python
import functools
import math

import jax
import jax.numpy as jnp
from jax import lax
from jax.experimental import pallas as pl
from jax.experimental.pallas import tpu as pltpu

F32 = jnp.float32
BF16 = jnp.bfloat16

DIFF_HEAD_DIM = 128
V_HEAD_DIM = 2 * DIFF_HEAD_DIM
ROPE_DIM = DIFF_HEAD_DIM // 4
ROPE_HALF = ROPE_DIM // 2
ROPE_THETA = 500000.0
SUBLN_EPS = 1e-5
NORM_EPS = 1e-5
POOL_WINDOWS = (2, 4, 8, 16)
MAX_WINDOW = max(POOL_WINDOWS)
CONV_WIDTH = 3
MASK_VALUE = -1e30

LANES = 128
MXU_WIDTH = 256
NORM_ROWS = 64
DOWN_SLAB = MXU_WIDTH
BF16_SUBLANES = 16
VMEM_LIMIT_BYTES = 58 * 1024 * 1024
HALO = BF16_SUBLANES

assert HALO >= MAX_WINDOW and HALO >= CONV_WIDTH - 1


def _tile(dim, pref):
    t = min(dim, pref)
    while dim % t:
        t -= 1
    return t


def _rms(x, g, eps):
    return x * lax.rsqrt(jnp.mean(x * x, axis=-1, keepdims=True) + eps) * g


def _rms_rows(src_ref, g_ref, dst_ref, dst_row0=0, copy_ref=None):
    def body(c, carry):
        r0 = pl.multiple_of(c * NORM_ROWS, NORM_ROWS)
        x = src_ref[pl.ds(r0, NORM_ROWS), :]
        if copy_ref is not None:
            copy_ref[pl.ds(r0, NORM_ROWS), :] = x
        dst_ref[pl.ds(dst_row0 + r0, NORM_ROWS), :] = _rms(x, g_ref[...], NORM_EPS).astype(dst_ref.dtype)
        return carry

    lax.fori_loop(0, src_ref.shape[0] // NORM_ROWS, body, 0)


def _params(*sem):
    return pltpu.CompilerParams(dimension_semantics=sem, vmem_limit_bytes=VMEM_LIMIT_BYTES)


def _inproj_kernel(x_ref, g_ref, w_ref, c_ref, sa_ref, sb_ref, qkv_ref, u_ref, n_ref, *, a_tiles, scale):
    j = pl.program_id(1)

    @pl.when(j == 0)
    def _():
        _rms_rows(x_ref, g_ref, n_ref)

    tn = w_ref.shape[1]
    sub = min(tn, MXU_WIDTH)

    def rope(t, mult):
        r = (t * c_ref[...] + pltpu.roll(t, LANES - ROPE_HALF, 1) * sa_ref[...]
             + pltpu.roll(t, ROPE_HALF, 1) * sb_ref[...])
        return r if mult is None else r * mult

    def project(store):
        for c0 in range(0, tn, sub):
            store(c0, jnp.dot(n_ref[...], w_ref[:, c0:c0 + sub], preferred_element_type=F32))

    def store_rope(mult, c0, acc):
        for k in range(0, sub, LANES):
            qkv_ref[:, c0 + k:c0 + k + LANES] = rope(acc[:, k:k + LANES], mult).astype(BF16)

    def store_v(c0, acc):
        qkv_ref[:, c0:c0 + sub] = acc.astype(BF16)

    def store_u(c0, acc):
        u_ref[:, c0:c0 + sub] = acc

    pl.when(j < a_tiles)(lambda: project(functools.partial(store_rope, scale)))
    pl.when((j >= a_tiles) & (j < 2 * a_tiles))(lambda: project(functools.partial(store_rope, None)))
    pl.when((j >= 2 * a_tiles) & (j < 3 * a_tiles))(lambda: project(store_v))
    pl.when(j >= 3 * a_tiles)(lambda: project(store_u))


def _inproj(h, g, w, rope_c, rope_sa, rope_sb, attn_width):
    t, d = h.shape
    n_out = w.shape[1]
    pool_width = n_out - 3 * attn_width
    tm = _tile(t, 512)
    tn = _tile(math.gcd(attn_width, pool_width), 1024)
    a_tiles = attn_width // tn
    qkv_last = 3 * a_tiles - 1
    return pl.pallas_call(
        functools.partial(_inproj_kernel, a_tiles=a_tiles, scale=DIFF_HEAD_DIM ** -0.5),
        grid=(t // tm, n_out // tn),
        in_specs=[
            pl.BlockSpec((tm, d), lambda i, j: (i, 0)),
            pl.BlockSpec((1, d), lambda i, j: (0, 0)),
            pl.BlockSpec((d, tn), lambda i, j: (0, j)),
            pl.BlockSpec((tm, LANES), lambda i, j: (i, 0)),
            pl.BlockSpec((tm, LANES), lambda i, j: (i, 0)),
            pl.BlockSpec((tm, LANES), lambda i, j: (i, 0)),
        ],
        out_specs=[
            pl.BlockSpec((tm, tn), lambda i, j: (i, jnp.minimum(j, qkv_last))),
            pl.BlockSpec((tm, tn), lambda i, j: (i, jnp.maximum(j - (qkv_last + 1), 0))),
        ],
        out_shape=[
            jax.ShapeDtypeStruct((t, 3 * attn_width), BF16),
            jax.ShapeDtypeStruct((t, pool_width), F32),
        ],
        scratch_shapes=[pltpu.VMEM((tm, d), BF16)],
        compiler_params=_params("parallel", "arbitrary"),
        name="inproj",
    )(h, g, w, rope_c, rope_sa, rope_sb)


def _attn_kernel(*refs, tq, tk, lambda_init, cast_blocks):
    n_cast = len(cast_blocks)
    lq1_ref, lk1_ref, lq2_ref, lk2_ref, sg_ref, q_ref, k_ref, v_ref = refs[:8]
    w32_refs = refs[8:8 + n_cast]
    o_ref = refs[8 + n_cast]
    w16_refs = refs[9 + n_cast:9 + 2 * n_cast]
    vt_ref, acc_ref, sa_ref, sb_ref, m_ref, l_ref = refs[9 + 2 * n_cast:]
    qi = pl.program_id(2)
    dh = DIFF_HEAD_DIM

    step_id = (pl.program_id(0) * pl.num_programs(1) + pl.program_id(1)) * pl.num_programs(2) + qi
    for w32_ref, w16_ref, n_blocks in zip(w32_refs, w16_refs, cast_blocks):
        def cast(w32_ref=w32_ref, w16_ref=w16_ref):
            w16_ref[...] = w32_ref[...].astype(BF16)
        pl.when(step_id < n_blocks)(cast)

    @pl.when(qi == 0)
    def _():
        for j in range(vt_ref.shape[0]):
            vt_ref[j] = v_ref[j * tk:(j + 1) * tk, :].T

    acc_ref[...] = jnp.zeros_like(acc_ref)
    for c in range(2):
        m_ref[c] = jnp.full((1, tq), MASK_VALUE, F32)
        l_ref[c] = jnp.zeros((1, tq), F32)

    def scores(j, dst_ref):
        kv0 = pl.multiple_of(j * tk, tk)
        for c in range(2):
            dst_ref[c] = lax.dot_general(
                k_ref[pl.ds(kv0, tk), c * dh:(c + 1) * dh], q_ref[:, c * dh:(c + 1) * dh],
                (((1,), (1,)), ((), ())), preferred_element_type=F32)

    def softmax_pv(j, src_ref, diag_offset=None):
        vt = vt_ref[j]
        for c in range(2):
            st = src_ref[c]
            if diag_offset is not None:
                key = lax.broadcasted_iota(jnp.int32, st.shape, 0) + diag_offset
                qry = lax.broadcasted_iota(jnp.int32, st.shape, 1)
                st = jnp.where(key <= qry, st, MASK_VALUE)
            m_prev = m_ref[c]
            m_new = jnp.maximum(m_prev, jnp.max(st, axis=0, keepdims=True))
            alpha = jnp.exp(m_prev - m_new)
            pt = jnp.exp(st - m_new)
            m_ref[c] = m_new
            l_ref[c] = alpha * l_ref[c] + jnp.sum(pt, axis=0, keepdims=True)
            acc_ref[c] = alpha * acc_ref[c] + jnp.dot(vt, pt.astype(BF16), preferred_element_type=F32)

    def step(j, src_ref, dst_ref, diag_offset=None):
        softmax_pv(j, src_ref, diag_offset)
        scores(j + 1, dst_ref)

    def pair(jj, carry):
        step(2 * jj, sa_ref, sb_ref)
        step(2 * jj + 1, sb_ref, sa_ref)
        return carry

    scores(0, sa_ref)
    lax.fori_loop(0, qi, pair, 0)
    step(2 * qi, sa_ref, sb_ref, diag_offset=0)
    softmax_pv(2 * qi + 1, sb_ref, diag_offset=tk)

    lam = (jnp.exp(jnp.sum(lq1_ref[...] * lk1_ref[...], axis=-1, keepdims=True))
           - jnp.exp(jnp.sum(lq2_ref[...] * lk2_ref[...], axis=-1, keepdims=True)) + lambda_init)
    ot = acc_ref[0] * (1.0 / l_ref[0]) - lam * (acc_ref[1] * (1.0 / l_ref[1]))
    yt = ot * lax.rsqrt(jnp.mean(ot * ot, axis=0, keepdims=True) + SUBLN_EPS) * sg_ref[...]
    o_ref[...] = (yt * (1.0 - lambda_init)).T.astype(BF16)


def _cast_rows(rows, n_steps):
    r = BF16_SUBLANES
    while r < rows and (rows % r or rows // r > n_steps):
        r += BF16_SUBLANES
    return r


def _attention(qkv, lq1, lk1, lq2, lk2, subln_g, batch, seq, attn_width, lambda_init, cast_weights):
    t = qkv.shape[0]
    heads = attn_width // V_HEAD_DIM
    tq = _tile(seq, 1024)
    tk = tq // 2
    assert tq == 2 * tk and seq % tk == 0
    nq = seq // tq
    vec = pl.BlockSpec((1, DIFF_HEAD_DIM), lambda b, h, i: (0, 0))
    scores = pltpu.VMEM((2, tk, tq), F32)
    stat = pltpu.VMEM((2, 1, tq), F32)
    n_steps = batch * heads * nq
    cast_rows = [_cast_rows(w.shape[0], n_steps) for w in cast_weights]
    cast_blocks = tuple(w.shape[0] // r for w, r in zip(cast_weights, cast_rows))
    assert all(n <= n_steps for n in cast_blocks)

    def cast_spec(w, r, n_blocks):
        return pl.BlockSpec((r, w.shape[1]),
                            lambda b, h, i: (jnp.minimum((b * heads + h) * nq + i, n_blocks - 1), 0))

    cast_specs = [cast_spec(w, r, n) for w, r, n in zip(cast_weights, cast_rows, cast_blocks)]
    return pl.pallas_call(
        functools.partial(_attn_kernel, tq=tq, tk=tk, lambda_init=lambda_init, cast_blocks=cast_blocks),
        grid=(batch, heads, nq),
        in_specs=[
            vec, vec, vec, vec,
            pl.BlockSpec((V_HEAD_DIM, 1), lambda b, h, i: (0, 0)),
            pl.BlockSpec((tq, V_HEAD_DIM), lambda b, h, i: (b * nq + i, h)),
            pl.BlockSpec((seq, V_HEAD_DIM), lambda b, h, i: (b, heads + h)),
            pl.BlockSpec((seq, V_HEAD_DIM), lambda b, h, i: (b, 2 * heads + h)),
        ] + cast_specs,
        out_specs=[pl.BlockSpec((tq, V_HEAD_DIM), lambda b, h, i: (b * nq + i, h))] + cast_specs,
        out_shape=[jax.ShapeDtypeStruct((t, attn_width), BF16)]
        + [jax.ShapeDtypeStruct(w.shape, BF16) for w in cast_weights],
        scratch_shapes=[
            pltpu.VMEM((seq // tk, V_HEAD_DIM, tk), BF16),
            pltpu.VMEM((2, V_HEAD_DIM, tq), F32),
            scores, scores,
            stat, stat,
        ],
        compiler_params=_params("arbitrary", "arbitrary", "arbitrary"),
        name="diff_attention",
    )(lq1, lk1, lq2, lk2, subln_g.reshape(V_HEAD_DIM, 1), qkv, qkv, qkv, *cast_weights)


def _pool_kernel(u_ref, w_ref, sc_ref, o_ref, *, seq, rows):
    g = pl.program_id(0)
    cols = u_ref.shape[1]

    def chunk(r0, ext, window):
        s, span = ext, 1
        while span < window:
            s = s + pltpu.roll(s, span, 0)
            span *= 2
        u = ext[HALO:]
        pos = r0 + lax.broadcasted_iota(jnp.int32, (rows, 1), 0)
        inv_count = 1.0 / jnp.minimum(pos + 1, window).astype(F32)
        pooled = s[HALO:] * inv_count - u
        y = jnp.dot(pooled.astype(BF16), w_ref[...], preferred_element_type=F32) * sc_ref[...]
        o_ref[pl.ds(r0, rows), :] = y.astype(BF16)

    def run(window):
        chunk(0, jnp.concatenate([jnp.zeros((HALO, cols), F32), u_ref[pl.ds(0, rows), :]], axis=0), window)

        def body(c, carry):
            r0 = pl.multiple_of(c * rows, rows)
            chunk(r0, u_ref[pl.ds(r0 - HALO, rows + HALO), :], window)
            return carry

        lax.fori_loop(1, seq // rows, body, 0)

    for gi, window in enumerate(POOL_WINDOWS):
        pl.when(g == gi)(functools.partial(run, window))


def _pooling(u, w_pool, pool_scale, batch, seq):
    t, width = u.shape
    groups = len(POOL_WINDOWS)
    cols = width // groups
    rows = _tile(seq, 512)
    return pl.pallas_call(
        functools.partial(_pool_kernel, seq=seq, rows=rows),
        grid=(groups, batch),
        in_specs=[
            pl.BlockSpec((seq, cols), lambda g, b: (b, g)),
            pl.BlockSpec((None, cols, cols), lambda g, b: (g, 0, 0)),
            pl.BlockSpec((1, cols), lambda g, b: (0, g)),
        ],
        out_specs=pl.BlockSpec((seq, cols), lambda g, b: (b, g)),
        out_shape=jax.ShapeDtypeStruct((t, width), BF16),
        compiler_params=_params("parallel", "parallel"),
        name="pool",
    )(u, w_pool, pool_scale)


def _outproj_kernel(a_ref, p_ref, wa_ref, wp_ref, x_ref, h_ref):
    acc = jnp.dot(a_ref[...], wa_ref[...], preferred_element_type=F32)
    acc = acc + jnp.dot(p_ref[...], wp_ref[...], preferred_element_type=F32)
    h_ref[...] = x_ref[...] + acc


def _outproj(attn, pool, w_out, x):
    t, d = x.shape
    aw, pw = attn.shape[1], pool.shape[1]
    assert aw == pw, "the two mixer groups are equally wide"
    tm = _tile(t, 512)
    tn = _tile(d, 1024)
    return pl.pallas_call(
        _outproj_kernel,
        grid=(t // tm, d // tn),
        in_specs=[
            pl.BlockSpec((tm, aw), lambda i, j: (i, 0)),
            pl.BlockSpec((tm, pw), lambda i, j: (i, 0)),
            pl.BlockSpec((aw, tn), lambda i, j: (0, j)),
            pl.BlockSpec((pw, tn), lambda i, j: (1, j)),
            pl.BlockSpec((tm, tn), lambda i, j: (i, j)),
        ],
        out_specs=pl.BlockSpec((tm, tn), lambda i, j: (i, j)),
        out_shape=jax.ShapeDtypeStruct((t, d), F32),
        compiler_params=_params("parallel", "parallel"),
        name="outproj",
    )(attn, pool, w_out, w_out, x)


def _mlp_kernel(h_ref, halo_ref, g2_ref, wg_ref, wu_ref, cw_ref, cb_ref, wd_ref, gf_ref, o_ref, n_ref,
                *, tm, tiles_per_seq, final_norm):
    i = pl.program_id(0)
    j = pl.program_id(1)

    @pl.when(j == 0)
    def _():
        keep = (i % tiles_per_seq != 0).astype(F32)
        n_ref[:HALO, :] = (_rms(halo_ref[...], g2_ref[...], NORM_EPS) * keep).astype(BF16)
        _rms_rows(h_ref, g2_ref, n_ref, dst_row0=HALO, copy_ref=o_ref)

    gate = jnp.dot(n_ref[...], wg_ref[...], preferred_element_type=F32)
    up = jnp.dot(n_ref[HALO:, :], wu_ref[...], preferred_element_type=F32)
    conv = cb_ref[...]
    for tap in range(CONV_WIDTH):
        start = HALO - (CONV_WIDTH - 1) + tap
        conv = conv + cw_ref[tap:tap + 1, :] * gate[start:start + tm]
    act = (jax.nn.silu(conv) * up).astype(BF16)
    for c0 in range(0, o_ref.shape[1], DOWN_SLAB):
        cols = slice(c0, c0 + DOWN_SLAB)
        o_ref[:, cols] += jnp.dot(act, wd_ref[:, cols], preferred_element_type=F32)

    if final_norm:
        @pl.when(j == pl.num_programs(1) - 1)
        def _():
            _rms_rows(o_ref, gf_ref, o_ref)


def _mlp(h, g2, w_gate, w_up, conv_w, conv_b, w_down, gf, seq, final_norm):
    t, d = h.shape
    f = w_gate.shape[1]
    tm = _tile(seq, 512)
    tf = _tile(f, 256)
    assert tm % HALO == 0
    halo_blocks = tm // HALO
    return pl.pallas_call(
        functools.partial(_mlp_kernel, tm=tm, tiles_per_seq=seq // tm, final_norm=final_norm),
        grid=(t // tm, f // tf),
        in_specs=[
            pl.BlockSpec((tm, d), lambda i, j: (i, 0)),
            pl.BlockSpec((HALO, d), lambda i, j: (jnp.maximum(i * halo_blocks - 1, 0), 0)),
            pl.BlockSpec((1, d), lambda i, j: (0, 0)),
            pl.BlockSpec((d, tf), lambda i, j: (0, j)),
            pl.BlockSpec((d, tf), lambda i, j: (0, j)),
            pl.BlockSpec((CONV_WIDTH, tf), lambda i, j: (0, j)),
            pl.BlockSpec((1, tf), lambda i, j: (0, j)),
            pl.BlockSpec((tf, d), lambda i, j: (j, 0)),
            pl.BlockSpec((1, d), lambda i, j: (0, 0)),
        ],
        out_specs=pl.BlockSpec((tm, d), lambda i, j: (i, 0)),
        out_shape=jax.ShapeDtypeStruct((t, d), F32),
        scratch_shapes=[pltpu.VMEM((HALO + tm, d), BF16)],
        compiler_params=_params("parallel", "arbitrary"),
        name="conv_mlp",
    )(h, h, g2, w_gate, w_up, conv_w, conv_b, w_down, gf)


def _rope_tables(positions):
    inv_freq = 1.0 / (ROPE_THETA ** (jnp.arange(0, ROPE_DIM, 2, dtype=F32) / ROPE_DIM))
    angles = positions.astype(F32).reshape(-1, 1) * inv_freq
    cos, sin = jnp.cos(angles), jnp.sin(angles)
    t = angles.shape[0]
    zeros = functools.partial(jnp.zeros, dtype=F32)
    c = jnp.concatenate([cos, cos, jnp.ones((t, LANES - ROPE_DIM), F32)], axis=1)
    sa = jnp.concatenate([-sin, zeros((t, LANES - ROPE_HALF))], axis=1)
    sb = jnp.concatenate([zeros((t, ROPE_HALF)), sin, zeros((t, LANES - ROPE_DIM))], axis=1)
    return c, sa, sb


def kernel(x, positions, norm1_g, w_in, lambda_q1, lambda_k1, lambda_q2, lambda_k2, subln_g, w_pool, pool_scale, w_out, norm2_g, w_gate, w_up, conv_w, conv_b, w_down, norm_f_g):
    batch, seq, d = x.shape
    depth = w_in.shape[0]
    pool_width = pool_scale.shape[1]
    attn_width = (w_in.shape[2] - pool_width) // 3
    rope_c, rope_sa, rope_sb = _rope_tables(positions)
    row = lambda v: v.reshape(1, -1)
    h = x.reshape(batch * seq, d)
    for l in range(depth):
        lambda_init = 0.8 - 0.6 * math.exp(-0.3 * l)
        qkv, u = _inproj(h, row(norm1_g[l]), w_in[l].astype(BF16), rope_c, rope_sa, rope_sb, attn_width)
        pool_w = w_pool[l]
        attn, w_gate_b, w_up_b, w_down_b, w_out_b, w_pool_b = _attention(
            qkv, row(lambda_q1[l]), row(lambda_k1[l]), row(lambda_q2[l]), row(lambda_k2[l]), row(subln_g[l]),
            batch, seq, attn_width, lambda_init,
            cast_weights=(w_gate[l], w_up[l], w_down[l], w_out[l], pool_w.reshape(-1, pool_w.shape[-1])))
        pool = _pooling(u, w_pool_b.reshape(pool_w.shape), row(pool_scale[l]), batch, seq)
        h = _outproj(attn, pool, w_out_b, h)
        h = _mlp(h, row(norm2_g[l]), w_gate_b, w_up_b, conv_w[l], row(conv_b[l]), w_down_b, row(norm_f_g), seq,
                 final_norm=(l == depth - 1))
    return h.reshape(batch, seq, d)
```

```python
import functools
import math

import jax
import jax.numpy as jnp
from jax import lax
from jax.experimental import pallas as pl
from jax.experimental.pallas import tpu as pltpu

F32 = jnp.float32
BF16 = jnp.bfloat16

DIFF_HEAD_DIM = 128
V_HEAD_DIM = 2 * DIFF_HEAD_DIM
ROPE_DIM = DIFF_HEAD_DIM // 4
ROPE_HALF = ROPE_DIM // 2
ROPE_THETA = 500000.0
SUBLN_EPS = 1e-5
NORM_EPS = 1e-5
POOL_WINDOWS = (2, 4, 8, 16)
MAX_WINDOW = max(POOL_WINDOWS)
CONV_WIDTH = 3
MASK_VALUE = -1e30

LANES = 128
MXU_WIDTH = 256
NORM_ROWS = 64
MLP_TILE = MXU_WIDTH
DOWN_SLAB = MXU_WIDTH
BF16_SUBLANES = 16
VMEM_LIMIT_BYTES = 58 * 1024 * 1024
HALO = BF16_SUBLANES

assert HALO >= MAX_WINDOW and HALO >= CONV_WIDTH - 1


def _tile(dim, pref):
    t = min(dim, pref)
    while dim % t:
        t -= 1
    return t


def _rms(x, g, eps):
    return x * lax.rsqrt(jnp.mean(x * x, axis=-1, keepdims=True) + eps) * g


def _rms_rows(src_ref, g_ref, dst_ref, dst_row0=0, copy_ref=None):
    def body(c, carry):
        r0 = pl.multiple_of(c * NORM_ROWS, NORM_ROWS)
        x = src_ref[pl.ds(r0, NORM_ROWS), :]
        if copy_ref is not None:
            copy_ref[pl.ds(r0, NORM_ROWS), :] = x
        dst_ref[pl.ds(dst_row0 + r0, NORM_ROWS), :] = _rms(x, g_ref[...], NORM_EPS).astype(dst_ref.dtype)
        return carry

    lax.fori_loop(0, src_ref.shape[0] // NORM_ROWS, body, 0)


def _params(*sem):
    return pltpu.CompilerParams(dimension_semantics=sem, vmem_limit_bytes=VMEM_LIMIT_BYTES)


def _inproj_kernel(x_ref, g_ref, w_ref, c_ref, sa_ref, sb_ref, qkv_ref, u_ref, n_ref, *, a_tiles, scale):
    j = pl.program_id(1)

    @pl.when(j == 0)
    def _():
        _rms_rows(x_ref, g_ref, n_ref)

    tn = w_ref.shape[1]
    sub = min(tn, MXU_WIDTH)

    def rope(t, mult):
        r = (t * c_ref[...] + pltpu.roll(t, LANES - ROPE_HALF, 1) * sa_ref[...]
             + pltpu.roll(t, ROPE_HALF, 1) * sb_ref[...])
        return r if mult is None else r * mult

    def project(store):
        for c0 in range(0, tn, sub):
            store(c0, jnp.dot(n_ref[...], w_ref[:, c0:c0 + sub], preferred_element_type=F32))

    def store_rope(mult, c0, acc):
        for k in range(0, sub, LANES):
            qkv_ref[:, c0 + k:c0 + k + LANES] = rope(acc[:, k:k + LANES], mult).astype(BF16)

    def store_v(c0, acc):
        qkv_ref[:, c0:c0 + sub] = acc.astype(BF16)

    def store_u(c0, acc):
        u_ref[:, c0:c0 + sub] = acc

    pl.when(j < a_tiles)(lambda: project(functools.partial(store_rope, scale)))
    pl.when((j >= a_tiles) & (j < 2 * a_tiles))(lambda: project(functools.partial(store_rope, None)))
    pl.when((j >= 2 * a_tiles) & (j < 3 * a_tiles))(lambda: project(store_v))
    pl.when(j >= 3 * a_tiles)(lambda: project(store_u))


def _inproj(h, g, w, rope_c, rope_sa, rope_sb, attn_width):
    t, d = h.shape
    n_out = w.shape[1]
    pool_width = n_out - 3 * attn_width
    tm = _tile(t, 512)
    tn = _tile(math.gcd(attn_width, pool_width), 1024)
    a_tiles = attn_width // tn
    qkv_last = 3 * a_tiles - 1
    return pl.pallas_call(
        functools.partial(_inproj_kernel, a_tiles=a_tiles, scale=DIFF_HEAD_DIM ** -0.5),
        grid=(t // tm, n_out // tn),
        in_specs=[
            pl.BlockSpec((tm, d), lambda i, j: (i, 0)),
            pl.BlockSpec((1, d), lambda i, j: (0, 0)),
            pl.BlockSpec((d, tn), lambda i, j: (0, j)),
            pl.BlockSpec((tm, LANES), lambda i, j: (i, 0)),
            pl.BlockSpec((tm, LANES), lambda i, j: (i, 0)),
            pl.BlockSpec((tm, LANES), lambda i, j: (i, 0)),
        ],
        out_specs=[
            pl.BlockSpec((tm, tn), lambda i, j: (i, jnp.minimum(j, qkv_last))),
            pl.BlockSpec((tm, tn), lambda i, j: (i, jnp.maximum(j - (qkv_last + 1), 0))),
        ],
        out_shape=[
            jax.ShapeDtypeStruct((t, 3 * attn_width), BF16),
            jax.ShapeDtypeStruct((t, pool_width), F32),
        ],
        scratch_shapes=[pltpu.VMEM((tm, d), BF16)],
        compiler_params=_params("parallel", "arbitrary"),
        name="inproj",
    )(h, g, w, rope_c, rope_sa, rope_sb)


def _attn_kernel(*refs, tq, tk, lambda_init, cast_blocks):
    n_cast = len(cast_blocks)
    lq1_ref, lk1_ref, lq2_ref, lk2_ref, sg_ref, q_ref, k_ref, v_ref = refs[:8]
    w32_refs = refs[8:8 + n_cast]
    o_ref = refs[8 + n_cast]
    w16_refs = refs[9 + n_cast:9 + 2 * n_cast]
    vt_ref, acc_ref, sa_ref, sb_ref, m_ref, l_ref = refs[9 + 2 * n_cast:]
    qi = pl.program_id(2)
    dh = DIFF_HEAD_DIM

    step_id = (pl.program_id(0) * pl.num_programs(1) + pl.program_id(1)) * pl.num_programs(2) + qi
    for w32_ref, w16_ref, n_blocks in zip(w32_refs, w16_refs, cast_blocks):
        def cast(w32_ref=w32_ref, w16_ref=w16_ref):
            w16_ref[...] = w32_ref[...].astype(BF16)
        pl.when(step_id < n_blocks)(cast)

    @pl.when(qi == 0)
    def _():
        for j in range(vt_ref.shape[0]):
            vt_ref[j] = v_ref[j * tk:(j + 1) * tk, :].T

    acc_ref[...] = jnp.zeros_like(acc_ref)
    for c in range(2):
        m_ref[c] = jnp.full((1, tq), MASK_VALUE, F32)
        l_ref[c] = jnp.zeros((1, tq), F32)

    def scores(j, dst_ref, q0=0):
        kv0 = pl.multiple_of(j * tk, tk)
        for c in range(2):
            dst_ref[c, :, q0:] = lax.dot_general(
                k_ref[pl.ds(kv0, tk), c * dh:(c + 1) * dh], q_ref[q0:, c * dh:(c + 1) * dh],
                (((1,), (1,)), ((), ())), preferred_element_type=F32)

    def softmax_pv(j, src_ref, diag_offset=None, q0=0):
        vt = vt_ref[j]
        for c in range(2):
            st = src_ref[c, :, q0:]
            if diag_offset is not None:
                key = lax.broadcasted_iota(jnp.int32, st.shape, 0) + diag_offset
                qry = lax.broadcasted_iota(jnp.int32, st.shape, 1) + q0
                st = jnp.where(key <= qry, st, MASK_VALUE)
            m_prev = m_ref[c, :, q0:]
            m_new = jnp.maximum(m_prev, jnp.max(st, axis=0, keepdims=True))
            alpha = jnp.exp(m_prev - m_new)
            pt = jnp.exp(st - m_new)
            m_ref[c, :, q0:] = m_new
            l_ref[c, :, q0:] = alpha * l_ref[c, :, q0:] + jnp.sum(pt, axis=0, keepdims=True)
            acc_ref[c, :, q0:] = (alpha * acc_ref[c, :, q0:]
                                  + jnp.dot(vt, pt.astype(BF16), preferred_element_type=F32))

    def step(j, src_ref, dst_ref):
        softmax_pv(j, src_ref)
        scores(j + 1, dst_ref)

    def pair(jj, carry):
        step(2 * jj, sa_ref, sb_ref)
        step(2 * jj + 1, sb_ref, sa_ref)
        return carry

    scores(0, sa_ref)
    lax.fori_loop(0, qi, pair, 0)
    softmax_pv(2 * qi, sa_ref, diag_offset=0)
    scores(2 * qi + 1, sb_ref, q0=tk)
    softmax_pv(2 * qi + 1, sb_ref, diag_offset=tk, q0=tk)

    lam = (jnp.exp(jnp.sum(lq1_ref[...] * lk1_ref[...], axis=-1, keepdims=True))
           - jnp.exp(jnp.sum(lq2_ref[...] * lk2_ref[...], axis=-1, keepdims=True)) + lambda_init)
    ot = acc_ref[0] * (1.0 / l_ref[0]) - lam * (acc_ref[1] * (1.0 / l_ref[1]))
    yt = ot * lax.rsqrt(jnp.mean(ot * ot, axis=0, keepdims=True) + SUBLN_EPS) * sg_ref[...]
    o_ref[...] = (yt * (1.0 - lambda_init)).T.astype(BF16)


def _cast_rows(rows, n_steps):
    r = BF16_SUBLANES
    while r < rows and (rows % r or rows // r > n_steps):
        r += BF16_SUBLANES
    return r


def _attention(qkv, lq1, lk1, lq2, lk2, subln_g, batch, seq, attn_width, lambda_init, cast_weights):
    t = qkv.shape[0]
    heads = attn_width // V_HEAD_DIM
    tq = _tile(seq, 1024)
    tk = tq // 2
    assert tq == 2 * tk and seq % tk == 0
    nq = seq // tq
    vec = pl.BlockSpec((1, DIFF_HEAD_DIM), lambda b, h, i: (0, 0))
    scores = pltpu.VMEM((2, tk, tq), F32)
    stat = pltpu.VMEM((2, 1, tq), F32)
    n_steps = batch * heads * nq
    cast_rows = [_cast_rows(w.shape[0], n_steps) for w in cast_weights]
    cast_blocks = tuple(w.shape[0] // r for w, r in zip(cast_weights, cast_rows))
    assert all(n <= n_steps for n in cast_blocks)

    def cast_spec(w, r, n_blocks):
        return pl.BlockSpec((r, w.shape[1]),
                            lambda b, h, i: (jnp.minimum((b * heads + h) * nq + i, n_blocks - 1), 0))

    cast_specs = [cast_spec(w, r, n) for w, r, n in zip(cast_weights, cast_rows, cast_blocks)]
    return pl.pallas_call(
        functools.partial(_attn_kernel, tq=tq, tk=tk, lambda_init=lambda_init, cast_blocks=cast_blocks),
        grid=(batch, heads, nq),
        in_specs=[
            vec, vec, vec, vec,
            pl.BlockSpec((V_HEAD_DIM, 1), lambda b, h, i: (0, 0)),
            pl.BlockSpec((tq, V_HEAD_DIM), lambda b, h, i: (b * nq + i, h)),
            pl.BlockSpec((seq, V_HEAD_DIM), lambda b, h, i: (b, heads + h)),
            pl.BlockSpec((seq, V_HEAD_DIM), lambda b, h, i: (b, 2 * heads + h)),
        ] + cast_specs,
        out_specs=[pl.BlockSpec((tq, V_HEAD_DIM), lambda b, h, i: (b * nq + i, h))] + cast_specs,
        out_shape=[jax.ShapeDtypeStruct((t, attn_width), BF16)]
        + [jax.ShapeDtypeStruct(w.shape, BF16) for w in cast_weights],
        scratch_shapes=[
            pltpu.VMEM((seq // tk, V_HEAD_DIM, tk), BF16),
            pltpu.VMEM((2, V_HEAD_DIM, tq), F32),
            scores, scores,
            stat, stat,
        ],
        compiler_params=_params("arbitrary", "arbitrary", "arbitrary"),
        name="diff_attention",
    )(lq1, lk1, lq2, lk2, subln_g.reshape(V_HEAD_DIM, 1), qkv, qkv, qkv, *cast_weights)


def _pool_kernel(u_ref, w_ref, sc_ref, o_ref, *, seq, rows):
    g = pl.program_id(0)
    cols = u_ref.shape[1]

    def chunk(r0, ext, window):
        s, span = ext, 1
        while span < window:
            s = s + pltpu.roll(s, span, 0)
            span *= 2
        u = ext[HALO:]
        pos = r0 + lax.broadcasted_iota(jnp.int32, (rows, 1), 0)
        inv_count = 1.0 / jnp.minimum(pos + 1, window).astype(F32)
        pooled = s[HALO:] * inv_count - u
        y = jnp.dot(pooled.astype(BF16), w_ref[...], preferred_element_type=F32) * sc_ref[...]
        o_ref[pl.ds(r0, rows), :] = y.astype(BF16)

    def run(window):
        chunk(0, jnp.concatenate([jnp.zeros((HALO, cols), F32), u_ref[pl.ds(0, rows), :]], axis=0), window)

        def body(c, carry):
            r0 = pl.multiple_of(c * rows, rows)
            chunk(r0, u_ref[pl.ds(r0 - HALO, rows + HALO), :], window)
            return carry

        lax.fori_loop(1, seq // rows, body, 0)

    for gi, window in enumerate(POOL_WINDOWS):
        pl.when(g == gi)(functools.partial(run, window))


def _pooling(u, w_pool, pool_scale, batch, seq):
    t, width = u.shape
    groups = len(POOL_WINDOWS)
    cols = width // groups
    rows = _tile(seq, 512)
    return pl.pallas_call(
        functools.partial(_pool_kernel, seq=seq, rows=rows),
        grid=(groups, batch),
        in_specs=[
            pl.BlockSpec((seq, cols), lambda g, b: (b, g)),
            pl.BlockSpec((None, cols, cols), lambda g, b: (g, 0, 0)),
            pl.BlockSpec((1, cols), lambda g, b: (0, g)),
        ],
        out_specs=pl.BlockSpec((seq, cols), lambda g, b: (b, g)),
        out_shape=jax.ShapeDtypeStruct((t, width), BF16),
        compiler_params=_params("parallel", "parallel"),
        name="pool",
    )(u, w_pool, pool_scale)


def _outproj_kernel(a_ref, p_ref, wa_ref, wp_ref, x_ref, h_ref):
    acc = jnp.dot(a_ref[...], wa_ref[...], preferred_element_type=F32)
    acc = acc + jnp.dot(p_ref[...], wp_ref[...], preferred_element_type=F32)
    h_ref[...] = x_ref[...] + acc


def _outproj(attn, pool, w_out, x):
    t, d = x.shape
    aw, pw = attn.shape[1], pool.shape[1]
    assert aw == pw, "the two mixer groups are equally wide"
    tm = _tile(t, 1024)
    tn = _tile(d, 1024)
    return pl.pallas_call(
        _outproj_kernel,
        grid=(t // tm, d // tn),
        in_specs=[
            pl.BlockSpec((tm, aw), lambda i, j: (i, 0)),
            pl.BlockSpec((tm, pw), lambda i, j: (i, 0)),
            pl.BlockSpec((aw, tn), lambda i, j: (0, j)),
            pl.BlockSpec((pw, tn), lambda i, j: (1, j)),
            pl.BlockSpec((tm, tn), lambda i, j: (i, j)),
        ],
        out_specs=pl.BlockSpec((tm, tn), lambda i, j: (i, j)),
        out_shape=jax.ShapeDtypeStruct((t, d), F32),
        compiler_params=_params("parallel", "parallel"),
        name="outproj",
    )(attn, pool, w_out, w_out, x)


def _mlp_kernel(h_ref, halo_ref, g2_ref, wg_ref, wu_ref, cw_ref, cb_ref, wd_ref, gf_ref, o_ref, n_ref,
                *, tm, tiles_per_seq, final_norm):
    i = pl.program_id(0)
    j = pl.program_id(1)

    @pl.when(j == 0)
    def _():
        keep = (i % tiles_per_seq != 0).astype(F32)
        n_ref[:HALO, :] = (_rms(halo_ref[...], g2_ref[...], NORM_EPS) * keep).astype(BF16)
        _rms_rows(h_ref, g2_ref, n_ref, dst_row0=HALO, copy_ref=o_ref)

    gate = jnp.dot(n_ref[...], wg_ref[...], preferred_element_type=F32)
    up = jnp.dot(n_ref[HALO:, :], wu_ref[...], preferred_element_type=F32)
    conv = cb_ref[...]
    for tap in range(CONV_WIDTH):
        lag = CONV_WIDTH - 1 - tap
        lagged = pltpu.roll(gate, lag, 0) if lag else gate
        conv = conv + cw_ref[tap:tap + 1, :] * lagged[HALO:]
    act = (jax.nn.silu(conv) * up).astype(BF16)
    for c0 in range(0, o_ref.shape[1], DOWN_SLAB):
        cols = slice(c0, c0 + DOWN_SLAB)
        o_ref[:, cols] += jnp.dot(act, wd_ref[:, cols], preferred_element_type=F32)

    if final_norm:
        @pl.when(j == pl.num_programs(1) - 1)
        def _():
            _rms_rows(o_ref, gf_ref, o_ref)


def _mlp(h, g2, w_gate, w_up, conv_w, conv_b, w_down, gf, seq, final_norm):
    t, d = h.shape
    f = w_gate.shape[1]
    tm = _tile(seq, 512)
    tf = _tile(f, MLP_TILE)
    assert tm % HALO == 0
    halo_blocks = tm // HALO
    return pl.pallas_call(
        functools.partial(_mlp_kernel, tm=tm, tiles_per_seq=seq // tm, final_norm=final_norm),
        grid=(t // tm, f // tf),
        in_specs=[
            pl.BlockSpec((tm, d), lambda i, j: (i, 0)),
            pl.BlockSpec((HALO, d), lambda i, j: (jnp.maximum(i * halo_blocks - 1, 0), 0)),
            pl.BlockSpec((1, d), lambda i, j: (0, 0)),
            pl.BlockSpec((d, tf), lambda i, j: (0, j)),
            pl.BlockSpec((d, tf), lambda i, j: (0, j)),
            pl.BlockSpec((CONV_WIDTH, tf), lambda i, j: (0, j)),
            pl.BlockSpec((1, tf), lambda i, j: (0, j)),
            pl.BlockSpec((tf, d), lambda i, j: (j, 0)),
            pl.BlockSpec((1, d), lambda i, j: (0, 0)),
        ],
        out_specs=pl.BlockSpec((tm, d), lambda i, j: (i, 0)),
        out_shape=jax.ShapeDtypeStruct((t, d), F32),
        scratch_shapes=[pltpu.VMEM((HALO + tm, d), BF16)],
        compiler_params=_params("parallel", "arbitrary"),
        name="conv_mlp",
    )(h, h, g2, w_gate, w_up, conv_w, conv_b, w_down, gf)


def _rope_tables(positions):
    inv_freq = 1.0 / (ROPE_THETA ** (jnp.arange(0, ROPE_DIM, 2, dtype=F32) / ROPE_DIM))
    angles = positions.astype(F32).reshape(-1, 1) * inv_freq
    cos, sin = jnp.cos(angles), jnp.sin(angles)
    t = angles.shape[0]
    zeros = functools.partial(jnp.zeros, dtype=F32)
    c = jnp.concatenate([cos, cos, jnp.ones((t, LANES - ROPE_DIM), F32)], axis=1)
    sa = jnp.concatenate([-sin, zeros((t, LANES - ROPE_HALF))], axis=1)
    sb = jnp.concatenate([zeros((t, ROPE_HALF)), sin, zeros((t, LANES - ROPE_DIM))], axis=1)
    return c, sa, sb


def kernel(x, positions, norm1_g, w_in, lambda_q1, lambda_k1, lambda_q2, lambda_k2, subln_g, w_pool, pool_scale, w_out, norm2_g, w_gate, w_up, conv_w, conv_b, w_down, norm_f_g):
    batch, seq, d = x.shape
    depth = w_in.shape[0]
    pool_width = pool_scale.shape[1]
    attn_width = (w_in.shape[2] - pool_width) // 3
    rope_c, rope_sa, rope_sb = _rope_tables(positions)
    row = lambda v: v.reshape(1, -1)
    h = x.reshape(batch * seq, d)
    for l in range(depth):
        lambda_init = 0.8 - 0.6 * math.exp(-0.3 * l)
        qkv, u = _inproj(h, row(norm1_g[l]), w_in[l].astype(BF16), rope_c, rope_sa, rope_sb, attn_width)
        pool_w = w_pool[l]
        attn, w_gate_b, w_up_b, w_down_b, w_out_b, w_pool_b = _attention(
            qkv, row(lambda_q1[l]), row(lambda_k1[l]), row(lambda_q2[l]), row(lambda_k2[l]), row(subln_g[l]),
            batch, seq, attn_width, lambda_init,
            cast_weights=(w_gate[l], w_up[l], w_down[l], w_out[l], pool_w.reshape(-1, pool_w.shape[-1])))
        pool = _pooling(u, w_pool_b.reshape(pool_w.shape), row(pool_scale[l]), batch, seq)
        h = _outproj(attn, pool, w_out_b, h)
        h = _mlp(h, row(norm2_g[l]), w_gate_b, w_up_b, conv_w[l], row(conv_b[l]), w_down_b, row(norm_f_g), seq,
                 final_norm=(l == depth - 1))
    return h.reshape(batch, seq, d)
```

```python
import functools
import math

import jax
import jax.numpy as jnp
from jax import lax
from jax.experimental import pallas as pl
from jax.experimental.pallas import tpu as pltpu

F32 = jnp.float32
BF16 = jnp.bfloat16

DIFF_HEAD_DIM = 128
V_HEAD_DIM = 2 * DIFF_HEAD_DIM
ROPE_DIM = DIFF_HEAD_DIM // 4
ROPE_HALF = ROPE_DIM // 2
ROPE_THETA = 500000.0
SUBLN_EPS = 1e-5
NORM_EPS = 1e-5
POOL_WINDOWS = (2, 4, 8, 16)
MAX_WINDOW = max(POOL_WINDOWS)
CONV_WIDTH = 3
MASK_VALUE = -1e30
LOG2_E = math.log2(math.e)

LANES = 128
MXU_WIDTH = 256
NORM_ROWS = 64
MLP_TILE = MXU_WIDTH
DOWN_SLAB = MXU_WIDTH
BF16_SUBLANES = 16
VMEM_LIMIT_BYTES = 58 * 1024 * 1024
HALO = BF16_SUBLANES

assert HALO >= MAX_WINDOW and HALO >= CONV_WIDTH - 1


def _tile(dim, pref):
    t = min(dim, pref)
    while dim % t:
        t -= 1
    return t


def _rms(x, g, eps):
    return x * lax.rsqrt(jnp.mean(x * x, axis=-1, keepdims=True) + eps) * g


def _rms_rows(src_ref, g_ref, dst_ref, dst_row0=0, copy_ref=None):
    def body(c, carry):
        r0 = pl.multiple_of(c * NORM_ROWS, NORM_ROWS)
        x = src_ref[pl.ds(r0, NORM_ROWS), :]
        if copy_ref is not None:
            copy_ref[pl.ds(r0, NORM_ROWS), :] = x
        dst_ref[pl.ds(dst_row0 + r0, NORM_ROWS), :] = _rms(x, g_ref[...], NORM_EPS).astype(dst_ref.dtype)
        return carry

    lax.fori_loop(0, src_ref.shape[0] // NORM_ROWS, body, 0)


def _params(*sem):
    return pltpu.CompilerParams(dimension_semantics=sem, vmem_limit_bytes=VMEM_LIMIT_BYTES)


def _inproj_kernel(x_ref, g_ref, w_ref, c_ref, sa_ref, sb_ref, qkv_ref, u_ref, n_ref, *, a_tiles, scale):
    j = pl.program_id(1)

    @pl.when(j == 0)
    def _():
        _rms_rows(x_ref, g_ref, n_ref)

    tn = w_ref.shape[1]
    sub = min(tn, MXU_WIDTH)

    def rope(t, mult):
        r = (t * c_ref[...] + pltpu.roll(t, LANES - ROPE_HALF, 1) * sa_ref[...]
             + pltpu.roll(t, ROPE_HALF, 1) * sb_ref[...])
        return r if mult is None else r * mult

    def project(store):
        for c0 in range(0, tn, sub):
            store(c0, jnp.dot(n_ref[...], w_ref[:, c0:c0 + sub], preferred_element_type=F32))

    def store_rope(mult, c0, acc):
        for k in range(0, sub, LANES):
            qkv_ref[:, c0 + k:c0 + k + LANES] = rope(acc[:, k:k + LANES], mult).astype(BF16)

    def store_v(c0, acc):
        qkv_ref[:, c0:c0 + sub] = acc.astype(BF16)

    def store_u(c0, acc):
        u_ref[:, c0:c0 + sub] = acc

    pl.when(j < a_tiles)(lambda: project(functools.partial(store_rope, scale)))
    pl.when((j >= a_tiles) & (j < 2 * a_tiles))(lambda: project(functools.partial(store_rope, None)))
    pl.when((j >= 2 * a_tiles) & (j < 3 * a_tiles))(lambda: project(store_v))
    pl.when(j >= 3 * a_tiles)(lambda: project(store_u))


def _inproj(h, g, w, rope_c, rope_sa, rope_sb, attn_width):
    t, d = h.shape
    n_out = w.shape[1]
    pool_width = n_out - 3 * attn_width
    tm = _tile(t, 512)
    tn = _tile(math.gcd(attn_width, pool_width), 1024)
    a_tiles = attn_width // tn
    qkv_last = 3 * a_tiles - 1
    return pl.pallas_call(
        functools.partial(_inproj_kernel, a_tiles=a_tiles, scale=DIFF_HEAD_DIM ** -0.5 * LOG2_E),
        grid=(t // tm, n_out // tn),
        in_specs=[
            pl.BlockSpec((tm, d), lambda i, j: (i, 0)),
            pl.BlockSpec((1, d), lambda i, j: (0, 0)),
            pl.BlockSpec((d, tn), lambda i, j: (0, j)),
            pl.BlockSpec((tm, LANES), lambda i, j: (i, 0)),
            pl.BlockSpec((tm, LANES), lambda i, j: (i, 0)),
            pl.BlockSpec((tm, LANES), lambda i, j: (i, 0)),
        ],
        out_specs=[
            pl.BlockSpec((tm, tn), lambda i, j: (i, jnp.minimum(j, qkv_last))),
            pl.BlockSpec((tm, tn), lambda i, j: (i, jnp.maximum(j - (qkv_last + 1), 0))),
        ],
        out_shape=[
            jax.ShapeDtypeStruct((t, 3 * attn_width), BF16),
            jax.ShapeDtypeStruct((t, pool_width), F32),
        ],
        scratch_shapes=[pltpu.VMEM((tm, d), BF16)],
        compiler_params=_params("parallel", "arbitrary"),
        name="inproj",
    )(h, g, w, rope_c, rope_sa, rope_sb)


def _attn_kernel(*refs, tq, tk, lambda_init, cast_blocks):
    n_cast = len(cast_blocks)
    lq1_ref, lk1_ref, lq2_ref, lk2_ref, sg_ref, q_ref, k_ref, v_ref = refs[:8]
    w32_refs = refs[8:8 + n_cast]
    o_ref = refs[8 + n_cast]
    w16_refs = refs[9 + n_cast:9 + 2 * n_cast]
    vt_ref, acc_ref, sa_ref, sb_ref, m_ref, l_ref = refs[9 + 2 * n_cast:]
    qi = pl.program_id(2)
    dh = DIFF_HEAD_DIM

    step_id = (pl.program_id(0) * pl.num_programs(1) + pl.program_id(1)) * pl.num_programs(2) + qi
    for w32_ref, w16_ref, n_blocks in zip(w32_refs, w16_refs, cast_blocks):
        def cast(w32_ref=w32_ref, w16_ref=w16_ref):
            if len(w16_ref.shape) == 3:
                tile = w16_ref.shape[2]
                for t in range(w16_ref.shape[0]):
                    w16_ref[t] = w32_ref[:, t * tile:(t + 1) * tile].astype(BF16)
            else:
                w16_ref[...] = w32_ref[...].astype(BF16)
        pl.when(step_id < n_blocks)(cast)

    @pl.when(qi == 0)
    def _():
        for j in range(vt_ref.shape[0]):
            vt_ref[j] = v_ref[j * tk:(j + 1) * tk, :].T

    acc_ref[...] = jnp.zeros_like(acc_ref)
    for c in range(2):
        m_ref[c] = jnp.full((1, tq), MASK_VALUE, F32)
        l_ref[c] = jnp.zeros((1, tq), F32)

    def scores(j, dst_ref, q0=0):
        kv0 = pl.multiple_of(j * tk, tk)
        for c in range(2):
            dst_ref[c, :, q0:] = lax.dot_general(
                k_ref[pl.ds(kv0, tk), c * dh:(c + 1) * dh], q_ref[q0:, c * dh:(c + 1) * dh],
                (((1,), (1,)), ((), ())), preferred_element_type=F32)

    def softmax_pv(j, src_ref, diag_offset=None, q0=0):
        vt = vt_ref[j]
        for c in range(2):
            st = src_ref[c, :, q0:]
            if diag_offset is not None:
                key = lax.broadcasted_iota(jnp.int32, st.shape, 0) + diag_offset
                qry = lax.broadcasted_iota(jnp.int32, st.shape, 1) + q0
                st = jnp.where(key <= qry, st, MASK_VALUE)
            m_prev = m_ref[c, :, q0:]
            m_new = jnp.maximum(m_prev, jnp.max(st, axis=0, keepdims=True))
            alpha = jnp.exp2(m_prev - m_new)
            pt = jnp.exp2(st - m_new)
            m_ref[c, :, q0:] = m_new
            l_ref[c, :, q0:] = alpha * l_ref[c, :, q0:] + jnp.sum(pt, axis=0, keepdims=True)
            acc_ref[c, :, q0:] = (alpha * acc_ref[c, :, q0:]
                                  + jnp.dot(vt, pt.astype(BF16), preferred_element_type=F32))

    def step(j, src_ref, dst_ref):
        softmax_pv(j, src_ref)
        scores(j + 1, dst_ref)

    def pair(jj, carry):
        step(2 * jj, sa_ref, sb_ref)
        step(2 * jj + 1, sb_ref, sa_ref)
        return carry

    scores(0, sa_ref)
    lax.fori_loop(0, qi, pair, 0)
    softmax_pv(2 * qi, sa_ref, diag_offset=0)
    scores(2 * qi + 1, sb_ref, q0=tk)
    softmax_pv(2 * qi + 1, sb_ref, diag_offset=tk, q0=tk)

    lam = (jnp.exp(jnp.sum(lq1_ref[...] * lk1_ref[...], axis=-1, keepdims=True))
           - jnp.exp(jnp.sum(lq2_ref[...] * lk2_ref[...], axis=-1, keepdims=True)) + lambda_init)
    ot = acc_ref[0] * (1.0 / l_ref[0]) - lam * (acc_ref[1] * (1.0 / l_ref[1]))
    yt = ot * lax.rsqrt(jnp.mean(ot * ot, axis=0, keepdims=True) + SUBLN_EPS) * sg_ref[...]
    o_ref[...] = (yt * (1.0 - lambda_init)).T.astype(BF16)


def _cast_rows(rows, n_steps):
    r = BF16_SUBLANES
    while r < rows and (rows % r or rows // r > n_steps):
        r += BF16_SUBLANES
    return r


def _attention(qkv, lq1, lk1, lq2, lk2, subln_g, batch, seq, attn_width, lambda_init, cast_weights, cast_tiles):
    t = qkv.shape[0]
    heads = attn_width // V_HEAD_DIM
    tq = _tile(seq, 1024)
    tk = tq // 2
    assert tq == 2 * tk and seq % tk == 0
    nq = seq // tq
    vec = pl.BlockSpec((1, DIFF_HEAD_DIM), lambda b, h, i: (0, 0))
    scores = pltpu.VMEM((2, tk, tq), F32)
    stat = pltpu.VMEM((2, 1, tq), F32)
    n_steps = batch * heads * nq
    cast_rows = [_cast_rows(w.shape[0], n_steps) for w in cast_weights]
    cast_blocks = tuple(w.shape[0] // r for w, r in zip(cast_weights, cast_rows))
    assert all(n <= n_steps for n in cast_blocks)

    def row_block(n_blocks):
        return lambda b, h, i: jnp.minimum((b * heads + h) * nq + i, n_blocks - 1)

    def in_spec(w, r, n_blocks):
        block = row_block(n_blocks)
        return pl.BlockSpec((r, w.shape[1]), lambda b, h, i: (block(b, h, i), 0))

    def out_spec(w, r, n_blocks, tile):
        block = row_block(n_blocks)
        if tile is None:
            return pl.BlockSpec((r, w.shape[1]), lambda b, h, i: (block(b, h, i), 0))
        return pl.BlockSpec((w.shape[1] // tile, r, tile), lambda b, h, i: (0, block(b, h, i), 0))

    def out_shape(w, tile):
        shape = w.shape if tile is None else (w.shape[1] // tile, w.shape[0], tile)
        return jax.ShapeDtypeStruct(shape, BF16)

    cast_in = [in_spec(w, r, n) for w, r, n in zip(cast_weights, cast_rows, cast_blocks)]
    cast_out = [out_spec(w, r, n, tl) for w, r, n, tl in zip(cast_weights, cast_rows, cast_blocks, cast_tiles)]
    return pl.pallas_call(
        functools.partial(_attn_kernel, tq=tq, tk=tk, lambda_init=lambda_init, cast_blocks=cast_blocks),
        grid=(batch, heads, nq),
        in_specs=[
            vec, vec, vec, vec,
            pl.BlockSpec((V_HEAD_DIM, 1), lambda b, h, i: (0, 0)),
            pl.BlockSpec((tq, V_HEAD_DIM), lambda b, h, i: (b * nq + i, h)),
            pl.BlockSpec((seq, V_HEAD_DIM), lambda b, h, i: (b, heads + h)),
            pl.BlockSpec((seq, V_HEAD_DIM), lambda b, h, i: (b, 2 * heads + h)),
        ] + cast_in,
        out_specs=[pl.BlockSpec((tq, V_HEAD_DIM), lambda b, h, i: (b * nq + i, h))] + cast_out,
        out_shape=[jax.ShapeDtypeStruct((t, attn_width), BF16)]
        + [out_shape(w, tl) for w, tl in zip(cast_weights, cast_tiles)],
        scratch_shapes=[
            pltpu.VMEM((seq // tk, V_HEAD_DIM, tk), BF16),
            pltpu.VMEM((2, V_HEAD_DIM, tq), F32),
            scores, scores,
            stat, stat,
        ],
        compiler_params=_params("arbitrary", "arbitrary", "arbitrary"),
        name="diff_attention",
    )(lq1, lk1, lq2, lk2, subln_g.reshape(V_HEAD_DIM, 1), qkv, qkv, qkv, *cast_weights)


def _pool_kernel(u_ref, w_ref, sc_ref, o_ref, *, seq, rows):
    g = pl.program_id(0)
    cols = u_ref.shape[1]

    def chunk(r0, ext, window):
        s, span = ext, 1
        while span < window:
            s = s + pltpu.roll(s, span, 0)
            span *= 2
        u = ext[HALO:]
        pos = r0 + lax.broadcasted_iota(jnp.int32, (rows, 1), 0)
        inv_count = 1.0 / jnp.minimum(pos + 1, window).astype(F32)
        pooled = s[HALO:] * inv_count - u
        y = jnp.dot(pooled.astype(BF16), w_ref[...], preferred_element_type=F32) * sc_ref[...]
        o_ref[pl.ds(r0, rows), :] = y.astype(BF16)

    def run(window):
        chunk(0, jnp.concatenate([jnp.zeros((HALO, cols), F32), u_ref[pl.ds(0, rows), :]], axis=0), window)

        def body(c, carry):
            r0 = pl.multiple_of(c * rows, rows)
            chunk(r0, u_ref[pl.ds(r0 - HALO, rows + HALO), :], window)
            return carry

        lax.fori_loop(1, seq // rows, body, 0)

    for gi, window in enumerate(POOL_WINDOWS):
        pl.when(g == gi)(functools.partial(run, window))


def _pooling(u, w_pool, pool_scale, batch, seq):
    t, width = u.shape
    groups = len(POOL_WINDOWS)
    cols = width // groups
    rows = _tile(seq, 512)
    return pl.pallas_call(
        functools.partial(_pool_kernel, seq=seq, rows=rows),
        grid=(groups, batch),
        in_specs=[
            pl.BlockSpec((seq, cols), lambda g, b: (b, g)),
            pl.BlockSpec((None, cols, cols), lambda g, b: (g, 0, 0)),
            pl.BlockSpec((1, cols), lambda g, b: (0, g)),
        ],
        out_specs=pl.BlockSpec((seq, cols), lambda g, b: (b, g)),
        out_shape=jax.ShapeDtypeStruct((t, width), BF16),
        compiler_params=_params("parallel", "parallel"),
        name="pool",
    )(u, w_pool, pool_scale)


def _outproj_kernel(a_ref, p_ref, wa_ref, wp_ref, x_ref, h_ref):
    acc = jnp.dot(a_ref[...], wa_ref[...], preferred_element_type=F32)
    acc = acc + jnp.dot(p_ref[...], wp_ref[...], preferred_element_type=F32)
    h_ref[...] = x_ref[...] + acc


def _outproj(attn, pool, w_out, x):
    t, d = x.shape
    aw, pw = attn.shape[1], pool.shape[1]
    assert aw == pw, "the two mixer groups are equally wide"
    tm = _tile(t, 1024)
    tn = _tile(d, 1024)
    return pl.pallas_call(
        _outproj_kernel,
        grid=(t // tm, d // tn),
        in_specs=[
            pl.BlockSpec((tm, aw), lambda i, j: (i, 0)),
            pl.BlockSpec((tm, pw), lambda i, j: (i, 0)),
            pl.BlockSpec((aw, tn), lambda i, j: (0, j)),
            pl.BlockSpec((pw, tn), lambda i, j: (1, j)),
            pl.BlockSpec((tm, tn), lambda i, j: (i, j)),
        ],
        out_specs=pl.BlockSpec((tm, tn), lambda i, j: (i, j)),
        out_shape=jax.ShapeDtypeStruct((t, d), F32),
        compiler_params=_params("parallel", "parallel"),
        name="outproj",
    )(attn, pool, w_out, w_out, x)


def _mlp_kernel(h_ref, halo_ref, g2_ref, wg_ref, wu_ref, cw_ref, cb_ref, wd_ref, gf_ref, o_ref, n_ref,
                *, tm, tiles_per_seq, final_norm):
    i = pl.program_id(0)
    j = pl.program_id(1)

    @pl.when(j == 0)
    def _():
        keep = (i % tiles_per_seq != 0).astype(F32)
        n_ref[:HALO, :] = (_rms(halo_ref[...], g2_ref[...], NORM_EPS) * keep).astype(BF16)
        _rms_rows(h_ref, g2_ref, n_ref, dst_row0=HALO, copy_ref=o_ref)

    gate = jnp.dot(n_ref[...], wg_ref[...], preferred_element_type=F32)
    up = jnp.dot(n_ref[HALO:, :], wu_ref[...], preferred_element_type=F32)
    conv = cb_ref[...]
    for tap in range(CONV_WIDTH):
        lag = CONV_WIDTH - 1 - tap
        lagged = pltpu.roll(gate, lag, 0) if lag else gate
        conv = conv + cw_ref[tap:tap + 1, :] * lagged[HALO:]
    act = (jax.nn.silu(conv) * up).astype(BF16)
    for c0 in range(0, o_ref.shape[1], DOWN_SLAB):
        cols = slice(c0, c0 + DOWN_SLAB)
        o_ref[:, cols] += jnp.dot(act, wd_ref[:, cols], preferred_element_type=F32)

    if final_norm:
        @pl.when(j == pl.num_programs(1) - 1)
        def _():
            _rms_rows(o_ref, gf_ref, o_ref)


def _mlp(h, g2, w_gate, w_up, conv_w, conv_b, w_down, gf, seq, final_norm):
    t, d = h.shape
    n_tiles, _, tf = w_gate.shape
    f = n_tiles * tf
    tm = _tile(seq, 512)
    assert tm % HALO == 0
    halo_blocks = tm // HALO
    return pl.pallas_call(
        functools.partial(_mlp_kernel, tm=tm, tiles_per_seq=seq // tm, final_norm=final_norm),
        grid=(t // tm, f // tf),
        in_specs=[
            pl.BlockSpec((tm, d), lambda i, j: (i, 0)),
            pl.BlockSpec((HALO, d), lambda i, j: (jnp.maximum(i * halo_blocks - 1, 0), 0)),
            pl.BlockSpec((1, d), lambda i, j: (0, 0)),
            pl.BlockSpec((None, d, tf), lambda i, j: (j, 0, 0)),
            pl.BlockSpec((None, d, tf), lambda i, j: (j, 0, 0)),
            pl.BlockSpec((CONV_WIDTH, tf), lambda i, j: (0, j)),
            pl.BlockSpec((1, tf), lambda i, j: (0, j)),
            pl.BlockSpec((tf, d), lambda i, j: (j, 0)),
            pl.BlockSpec((1, d), lambda i, j: (0, 0)),
        ],
        out_specs=pl.BlockSpec((tm, d), lambda i, j: (i, 0)),
        out_shape=jax.ShapeDtypeStruct((t, d), F32),
        scratch_shapes=[pltpu.VMEM((HALO + tm, d), BF16)],
        compiler_params=_params("parallel", "arbitrary"),
        name="conv_mlp",
    )(h, h, g2, w_gate, w_up, conv_w, conv_b, w_down, gf)


def _rope_tables(positions):
    inv_freq = 1.0 / (ROPE_THETA ** (jnp.arange(0, ROPE_DIM, 2, dtype=F32) / ROPE_DIM))
    angles = positions.astype(F32).reshape(-1, 1) * inv_freq
    cos, sin = lax.optimization_barrier((jnp.cos(angles), jnp.sin(angles)))
    t = angles.shape[0]
    zeros = functools.partial(jnp.zeros, dtype=F32)
    c = jnp.concatenate([cos, cos, jnp.ones((t, LANES - ROPE_DIM), F32)], axis=1)
    sa = jnp.concatenate([-sin, zeros((t, LANES - ROPE_HALF))], axis=1)
    sb = jnp.concatenate([zeros((t, ROPE_HALF)), sin, zeros((t, LANES - ROPE_DIM))], axis=1)
    return c, sa, sb


def kernel(x, positions, norm1_g, w_in, lambda_q1, lambda_k1, lambda_q2, lambda_k2, subln_g, w_pool, pool_scale, w_out, norm2_g, w_gate, w_up, conv_w, conv_b, w_down, norm_f_g):
    batch, seq, d = x.shape
    depth = w_in.shape[0]
    pool_width = pool_scale.shape[1]
    attn_width = (w_in.shape[2] - pool_width) // 3
    rope_c, rope_sa, rope_sb = _rope_tables(positions)
    row = lambda v: v.reshape(1, -1)
    h = x.reshape(batch * seq, d)
    for l in range(depth):
        lambda_init = 0.8 - 0.6 * math.exp(-0.3 * l)
        qkv, u = _inproj(h, row(norm1_g[l]), w_in[l].astype(BF16), rope_c, rope_sa, rope_sb, attn_width)
        pool_w = w_pool[l]
        mlp_tile = _tile(w_gate.shape[2], MLP_TILE)
        attn, w_gate_b, w_up_b, w_down_b, w_out_b, w_pool_b = _attention(
            qkv, row(lambda_q1[l]), row(lambda_k1[l]), row(lambda_q2[l]), row(lambda_k2[l]), row(subln_g[l]),
            batch, seq, attn_width, lambda_init,
            cast_weights=(w_gate[l], w_up[l], w_down[l], w_out[l], pool_w.reshape(-1, pool_w.shape[-1])),
            cast_tiles=(mlp_tile, mlp_tile, None, None, None))
        pool = _pooling(u, w_pool_b.reshape(pool_w.shape), row(pool_scale[l]), batch, seq)
        h = _outproj(attn, pool, w_out_b, h)
        h = _mlp(h, row(norm2_g[l]), w_gate_b, w_up_b, conv_w[l], row(conv_b[l]), w_down_b, row(norm_f_g), seq,
                 final_norm=(l == depth - 1))
    return h.reshape(batch, seq, d)
```

```python
import functools
import math

import jax
import jax.numpy as jnp
from jax import lax
from jax.experimental import pallas as pl
from jax.experimental.pallas import tpu as pltpu

F32 = jnp.float32
BF16 = jnp.bfloat16

DIFF_HEAD_DIM = 128
V_HEAD_DIM = 2 * DIFF_HEAD_DIM
ROPE_DIM = DIFF_HEAD_DIM // 4
ROPE_HALF = ROPE_DIM // 2
ROPE_THETA = 500000.0
SUBLN_EPS = 1e-5
NORM_EPS = 1e-5
POOL_WINDOWS = (2, 4, 8, 16)
MAX_WINDOW = max(POOL_WINDOWS)
CONV_WIDTH = 3
MASK_VALUE = -1e30
LOG2_E = math.log2(math.e)

LANES = 128
MXU_WIDTH = 256
NORM_ROWS = 64
MLP_TILE = MXU_WIDTH
DOWN_SLAB = MXU_WIDTH
BF16_SUBLANES = 16
VMEM_LIMIT_BYTES = 58 * 1024 * 1024
HALO = BF16_SUBLANES

assert HALO >= MAX_WINDOW and HALO >= CONV_WIDTH - 1


def _tile(dim, pref):
    t = min(dim, pref)
    while dim % t:
        t -= 1
    return t


def _rms(x, g, eps):
    return x * lax.rsqrt(jnp.mean(x * x, axis=-1, keepdims=True) + eps) * g


def _rms_rows(src_ref, g_ref, dst_ref, dst_row0=0, copy_ref=None):
    def body(c, carry):
        r0 = pl.multiple_of(c * NORM_ROWS, NORM_ROWS)
        x = src_ref[pl.ds(r0, NORM_ROWS), :]
        if copy_ref is not None:
            copy_ref[pl.ds(r0, NORM_ROWS), :] = x
        dst_ref[pl.ds(dst_row0 + r0, NORM_ROWS), :] = _rms(x, g_ref[...], NORM_EPS).astype(dst_ref.dtype)
        return carry

    lax.fori_loop(0, src_ref.shape[0] // NORM_ROWS, body, 0)


def _params(*sem):
    return pltpu.CompilerParams(dimension_semantics=sem, vmem_limit_bytes=VMEM_LIMIT_BYTES)


def _inproj_kernel(x_ref, g_ref, w_ref, c_ref, sa_ref, sb_ref, qkv_ref, u_ref, n_ref, *, a_tiles, scale):
    j = pl.program_id(1)

    @pl.when(j == 0)
    def _():
        _rms_rows(x_ref, g_ref, n_ref)

    tn = w_ref.shape[1]
    sub = min(tn, MXU_WIDTH)

    def rope(t, mult):
        r = (t * c_ref[...] + pltpu.roll(t, LANES - ROPE_HALF, 1) * sa_ref[...]
             + pltpu.roll(t, ROPE_HALF, 1) * sb_ref[...])
        return r if mult is None else r * mult

    def project(store):
        for c0 in range(0, tn, sub):
            store(c0, jnp.dot(n_ref[...], w_ref[:, c0:c0 + sub], preferred_element_type=F32))

    def store_rope(mult, c0, acc):
        for k in range(0, sub, LANES):
            qkv_ref[:, c0 + k:c0 + k + LANES] = rope(acc[:, k:k + LANES], mult).astype(BF16)

    def store_v(c0, acc):
        qkv_ref[:, c0:c0 + sub] = acc.astype(BF16)

    def store_u(c0, acc):
        u_ref[:, c0:c0 + sub] = acc

    pl.when(j < a_tiles)(lambda: project(functools.partial(store_rope, scale)))
    pl.when((j >= a_tiles) & (j < 2 * a_tiles))(lambda: project(functools.partial(store_rope, None)))
    pl.when((j >= 2 * a_tiles) & (j < 3 * a_tiles))(lambda: project(store_v))
    pl.when(j >= 3 * a_tiles)(lambda: project(store_u))


def _inproj(h, g, w, rope_c, rope_sa, rope_sb, attn_width):
    t, d = h.shape
    n_out = w.shape[1]
    pool_width = n_out - 3 * attn_width
    tm = _tile(t, 512)
    tn = _tile(math.gcd(attn_width, pool_width), 1024)
    a_tiles = attn_width // tn
    qkv_last = 3 * a_tiles - 1
    return pl.pallas_call(
        functools.partial(_inproj_kernel, a_tiles=a_tiles, scale=DIFF_HEAD_DIM ** -0.5 * LOG2_E),
        grid=(t // tm, n_out // tn),
        in_specs=[
            pl.BlockSpec((tm, d), lambda i, j: (i, 0)),
            pl.BlockSpec((1, d), lambda i, j: (0, 0)),
            pl.BlockSpec((d, tn), lambda i, j: (0, j)),
            pl.BlockSpec((tm, LANES), lambda i, j: (i, 0)),
            pl.BlockSpec((tm, LANES), lambda i, j: (i, 0)),
            pl.BlockSpec((tm, LANES), lambda i, j: (i, 0)),
        ],
        out_specs=[
            pl.BlockSpec((tm, tn), lambda i, j: (i, jnp.minimum(j, qkv_last))),
            pl.BlockSpec((tm, tn), lambda i, j: (i, jnp.maximum(j - (qkv_last + 1), 0))),
        ],
        out_shape=[
            jax.ShapeDtypeStruct((t, 3 * attn_width), BF16),
            jax.ShapeDtypeStruct((t, pool_width), F32),
        ],
        scratch_shapes=[pltpu.VMEM((tm, d), BF16)],
        compiler_params=_params("parallel", "arbitrary"),
        name="inproj",
    )(h, g, w, rope_c, rope_sa, rope_sb)


def _attn_kernel(*refs, tq, tk, lambda_init, cast_blocks):
    n_cast = len(cast_blocks)
    lq1_ref, lk1_ref, lq2_ref, lk2_ref, sg_ref, q_ref, k_ref, v_ref = refs[:8]
    w32_refs = refs[8:8 + n_cast]
    o_ref = refs[8 + n_cast]
    w16_refs = refs[9 + n_cast:9 + 2 * n_cast]
    vt_ref, acc_ref, sa_ref, sb_ref, m_ref, l_ref = refs[9 + 2 * n_cast:]
    qi = pl.program_id(2)
    dh = DIFF_HEAD_DIM

    step_id = (pl.program_id(0) * pl.num_programs(1) + pl.program_id(1)) * pl.num_programs(2) + qi
    for w32_ref, w16_ref, n_blocks in zip(w32_refs, w16_refs, cast_blocks):
        def cast(w32_ref=w32_ref, w16_ref=w16_ref):
            if len(w16_ref.shape) == 3:
                tile = w16_ref.shape[2]
                for t in range(w16_ref.shape[0]):
                    w16_ref[t] = w32_ref[:, t * tile:(t + 1) * tile].astype(BF16)
            else:
                w16_ref[...] = w32_ref[...].astype(BF16)
        pl.when(step_id < n_blocks)(cast)

    @pl.when(qi == 0)
    def _():
        for j in range(vt_ref.shape[0]):
            vt_ref[j] = v_ref[j * tk:(j + 1) * tk, :].T

    acc_ref[...] = jnp.zeros_like(acc_ref)
    for c in range(2):
        m_ref[c] = jnp.full((1, tq), MASK_VALUE, F32)
        l_ref[c] = jnp.zeros((1, tq), F32)

    def scores(j, dst_ref, q0=0):
        kv0 = pl.multiple_of(j * tk, tk)
        for c in range(2):
            dst_ref[c, :, q0:] = lax.dot_general(
                k_ref[pl.ds(kv0, tk), c * dh:(c + 1) * dh], q_ref[q0:, c * dh:(c + 1) * dh],
                (((1,), (1,)), ((), ())), preferred_element_type=F32)

    def softmax_pv(j, src_ref, diag_offset=None, q0=0):
        vt = vt_ref[j]
        for c in range(2):
            st = src_ref[c, :, q0:]
            if diag_offset is not None:
                key = lax.broadcasted_iota(jnp.int32, st.shape, 0) + diag_offset
                qry = lax.broadcasted_iota(jnp.int32, st.shape, 1) + q0
                st = jnp.where(key <= qry, st, MASK_VALUE)
            m_prev = m_ref[c, :, q0:]
            m_new = jnp.maximum(m_prev, jnp.max(st, axis=0, keepdims=True))
            alpha = jnp.exp2(m_prev - m_new)
            pt = jnp.exp2(st - m_new)
            m_ref[c, :, q0:] = m_new
            l_ref[c, :, q0:] = alpha * l_ref[c, :, q0:] + jnp.sum(pt, axis=0, keepdims=True)
            acc_ref[c, :, q0:] = (alpha * acc_ref[c, :, q0:]
                                  + jnp.dot(vt, pt.astype(BF16), preferred_element_type=F32))

    def step(j, src_ref, dst_ref):
        softmax_pv(j, src_ref)
        scores(j + 1, dst_ref)

    def pair(jj, carry):
        step(2 * jj, sa_ref, sb_ref)
        step(2 * jj + 1, sb_ref, sa_ref)
        return carry

    scores(0, sa_ref)
    lax.fori_loop(0, qi, pair, 0)
    softmax_pv(2 * qi, sa_ref, diag_offset=0)
    scores(2 * qi + 1, sb_ref, q0=tk)
    softmax_pv(2 * qi + 1, sb_ref, diag_offset=tk, q0=tk)

    lam = (jnp.exp(jnp.sum(lq1_ref[...] * lk1_ref[...], axis=-1, keepdims=True))
           - jnp.exp(jnp.sum(lq2_ref[...] * lk2_ref[...], axis=-1, keepdims=True)) + lambda_init)
    ot = acc_ref[0] * (1.0 / l_ref[0]) - lam * (acc_ref[1] * (1.0 / l_ref[1]))
    yt = ot * lax.rsqrt(jnp.mean(ot * ot, axis=0, keepdims=True) + SUBLN_EPS) * sg_ref[...]
    o_ref[...] = (yt * (1.0 - lambda_init)).T.astype(BF16)


def _cast_rows(rows, n_steps):
    r = BF16_SUBLANES
    while r < rows and (rows % r or rows // r > n_steps):
        r += BF16_SUBLANES
    return r


def _attention(qkv, lq1, lk1, lq2, lk2, subln_g, batch, seq, attn_width, lambda_init, cast_weights, cast_tiles):
    t = qkv.shape[0]
    heads = attn_width // V_HEAD_DIM
    tq = _tile(seq, 1024)
    tk = tq // 2
    assert tq == 2 * tk and seq % tk == 0
    nq = seq // tq
    vec = pl.BlockSpec((1, DIFF_HEAD_DIM), lambda b, h, i: (0, 0))
    scores = pltpu.VMEM((2, tk, tq), F32)
    stat = pltpu.VMEM((2, 1, tq), F32)
    n_steps = batch * heads * nq
    cast_rows = [_cast_rows(w.shape[0], n_steps) for w in cast_weights]
    cast_blocks = tuple(w.shape[0] // r for w, r in zip(cast_weights, cast_rows))
    assert all(n <= n_steps for n in cast_blocks)

    def row_block(n_blocks):
        return lambda b, h, i: jnp.minimum((b * heads + h) * nq + i, n_blocks - 1)

    def in_spec(w, r, n_blocks):
        block = row_block(n_blocks)
        return pl.BlockSpec((r, w.shape[1]), lambda b, h, i: (block(b, h, i), 0))

    def out_spec(w, r, n_blocks, tile):
        block = row_block(n_blocks)
        if tile is None:
            return pl.BlockSpec((r, w.shape[1]), lambda b, h, i: (block(b, h, i), 0))
        return pl.BlockSpec((w.shape[1] // tile, r, tile), lambda b, h, i: (0, block(b, h, i), 0))

    def out_shape(w, tile):
        shape = w.shape if tile is None else (w.shape[1] // tile, w.shape[0], tile)
        return jax.ShapeDtypeStruct(shape, BF16)

    cast_in = [in_spec(w, r, n) for w, r, n in zip(cast_weights, cast_rows, cast_blocks)]
    cast_out = [out_spec(w, r, n, tl) for w, r, n, tl in zip(cast_weights, cast_rows, cast_blocks, cast_tiles)]
    return pl.pallas_call(
        functools.partial(_attn_kernel, tq=tq, tk=tk, lambda_init=lambda_init, cast_blocks=cast_blocks),
        grid=(batch, heads, nq),
        in_specs=[
            vec, vec, vec, vec,
            pl.BlockSpec((V_HEAD_DIM, 1), lambda b, h, i: (0, 0)),
            pl.BlockSpec((tq, V_HEAD_DIM), lambda b, h, i: (b * nq + i, h)),
            pl.BlockSpec((seq, V_HEAD_DIM), lambda b, h, i: (b, heads + h)),
            pl.BlockSpec((seq, V_HEAD_DIM), lambda b, h, i: (b, 2 * heads + h)),
        ] + cast_in,
        out_specs=[pl.BlockSpec((tq, V_HEAD_DIM), lambda b, h, i: (b * nq + i, h))] + cast_out,
        out_shape=[jax.ShapeDtypeStruct((t, attn_width), BF16)]
        + [out_shape(w, tl) for w, tl in zip(cast_weights, cast_tiles)],
        scratch_shapes=[
            pltpu.VMEM((seq // tk, V_HEAD_DIM, tk), BF16),
            pltpu.VMEM((2, V_HEAD_DIM, tq), F32),
            scores, scores,
            stat, stat,
        ],
        compiler_params=_params("arbitrary", "arbitrary", "arbitrary"),
        name="diff_attention",
    )(lq1, lk1, lq2, lk2, subln_g.reshape(V_HEAD_DIM, 1), qkv, qkv, qkv, *cast_weights)


def _pool_kernel(u_ref, w_ref, sc_ref, o_ref, *, seq, rows):
    g = pl.program_id(0)
    cols = u_ref.shape[1]

    def chunk(r0, ext, window):
        s, span = ext, 1
        while span < window:
            s = s + pltpu.roll(s, span, 0)
            span *= 2
        u = ext[HALO:]
        pos = r0 + lax.broadcasted_iota(jnp.int32, (rows, 1), 0)
        inv_count = 1.0 / jnp.minimum(pos + 1, window).astype(F32)
        pooled = s[HALO:] * inv_count - u
        y = jnp.dot(pooled.astype(BF16), w_ref[...], preferred_element_type=F32) * sc_ref[...]
        o_ref[pl.ds(r0, rows), :] = y.astype(BF16)

    def run(window):
        chunk(0, jnp.concatenate([jnp.zeros((HALO, cols), F32), u_ref[pl.ds(0, rows), :]], axis=0), window)

        def body(c, carry):
            r0 = pl.multiple_of(c * rows, rows)
            chunk(r0, u_ref[pl.ds(r0 - HALO, rows + HALO), :], window)
            return carry

        lax.fori_loop(1, seq // rows, body, 0)

    for gi, window in enumerate(POOL_WINDOWS):
        pl.when(g == gi)(functools.partial(run, window))


def _pooling(u, w_pool, pool_scale, batch, seq):
    t, width = u.shape
    groups = len(POOL_WINDOWS)
    cols = width // groups
    rows = _tile(seq, 512)
    return pl.pallas_call(
        functools.partial(_pool_kernel, seq=seq, rows=rows),
        grid=(groups, batch),
        in_specs=[
            pl.BlockSpec((seq, cols), lambda g, b: (b, g)),
            pl.BlockSpec((None, cols, cols), lambda g, b: (g, 0, 0)),
            pl.BlockSpec((1, cols), lambda g, b: (0, g)),
        ],
        out_specs=pl.BlockSpec((seq, cols), lambda g, b: (b, g)),
        out_shape=jax.ShapeDtypeStruct((t, width), BF16),
        compiler_params=_params("parallel", "parallel"),
        name="pool",
    )(u, w_pool, pool_scale)


def _outproj_kernel(a_ref, p_ref, wa_ref, wp_ref, x_ref, h_ref):
    acc = jnp.dot(a_ref[...], wa_ref[...], preferred_element_type=F32)
    acc = acc + jnp.dot(p_ref[...], wp_ref[...], preferred_element_type=F32)
    h_ref[...] = x_ref[...] + acc


def _outproj(attn, pool, w_out, x):
    t, d = x.shape
    aw, pw = attn.shape[1], pool.shape[1]
    assert aw == pw, "the two mixer groups are equally wide"
    tm = _tile(t, 1024)
    tn = _tile(d, 1024)
    return pl.pallas_call(
        _outproj_kernel,
        grid=(t // tm, d // tn),
        in_specs=[
            pl.BlockSpec((tm, aw), lambda i, j: (i, 0)),
            pl.BlockSpec((tm, pw), lambda i, j: (i, 0)),
            pl.BlockSpec((aw, tn), lambda i, j: (0, j)),
            pl.BlockSpec((pw, tn), lambda i, j: (1, j)),
            pl.BlockSpec((tm, tn), lambda i, j: (i, j)),
        ],
        out_specs=pl.BlockSpec((tm, tn), lambda i, j: (i, j)),
        out_shape=jax.ShapeDtypeStruct((t, d), F32),
        compiler_params=_params("parallel", "parallel"),
        name="outproj",
    )(attn, pool, w_out, w_out, x)


def _mlp_kernel(h_ref, halo_ref, g2_ref, wg_ref, wu_ref, cw_ref, cb_ref, wd_ref, gf_ref, o_ref, n_ref,
                *, tm, tiles_per_seq, final_norm):
    i = pl.program_id(0)
    j = pl.program_id(1)

    @pl.when(j == 0)
    def _():
        keep = (i % tiles_per_seq != 0).astype(F32)
        n_ref[:HALO, :] = (_rms(halo_ref[...], g2_ref[...], NORM_EPS) * keep).astype(BF16)
        _rms_rows(h_ref, g2_ref, n_ref, dst_row0=HALO, copy_ref=o_ref)

    gate = jnp.dot(n_ref[...], wg_ref[...], preferred_element_type=F32)
    up = jnp.dot(n_ref[HALO:, :], wu_ref[...], preferred_element_type=F32)
    cw = cw_ref[j]
    conv = cb_ref[j]
    for tap in range(CONV_WIDTH):
        lag = CONV_WIDTH - 1 - tap
        lagged = pltpu.roll(gate, lag, 0) if lag else gate
        conv = conv + cw[tap:tap + 1, :] * lagged[HALO:]
    act = (jax.nn.silu(conv) * up).astype(BF16)
    for c0 in range(0, o_ref.shape[1], DOWN_SLAB):
        cols = slice(c0, c0 + DOWN_SLAB)
        o_ref[:, cols] += jnp.dot(act, wd_ref[:, cols], preferred_element_type=F32)

    if final_norm:
        @pl.when(j == pl.num_programs(1) - 1)
        def _():
            _rms_rows(o_ref, gf_ref, o_ref)


def _mlp(h, g2, w_gate, w_up, conv_w, conv_b, w_down, gf, seq, final_norm):
    t, d = h.shape
    n_tiles, _, tf = w_gate.shape
    f = n_tiles * tf
    tm = _tile(seq, 512)
    assert tm % HALO == 0
    halo_blocks = tm // HALO
    return pl.pallas_call(
        functools.partial(_mlp_kernel, tm=tm, tiles_per_seq=seq // tm, final_norm=final_norm),
        grid=(t // tm, f // tf),
        in_specs=[
            pl.BlockSpec((tm, d), lambda i, j: (i, 0)),
            pl.BlockSpec((HALO, d), lambda i, j: (jnp.maximum(i * halo_blocks - 1, 0), 0)),
            pl.BlockSpec((1, d), lambda i, j: (0, 0)),
            pl.BlockSpec((None, d, tf), lambda i, j: (j, 0, 0)),
            pl.BlockSpec((None, d, tf), lambda i, j: (j, 0, 0)),
            pl.BlockSpec((n_tiles, CONV_WIDTH, tf), lambda i, j: (0, 0, 0)),
            pl.BlockSpec((n_tiles, 1, tf), lambda i, j: (0, 0, 0)),
            pl.BlockSpec((tf, d), lambda i, j: (j, 0)),
            pl.BlockSpec((1, d), lambda i, j: (0, 0)),
        ],
        out_specs=pl.BlockSpec((tm, d), lambda i, j: (i, 0)),
        out_shape=jax.ShapeDtypeStruct((t, d), F32),
        scratch_shapes=[pltpu.VMEM((HALO + tm, d), BF16)],
        compiler_params=_params("parallel", "arbitrary"),
        name="conv_mlp",
    )(h, h, g2, w_gate, w_up, conv_w.reshape(CONV_WIDTH, n_tiles, tf).transpose(1, 0, 2),
      conv_b.reshape(n_tiles, 1, tf), w_down, gf)


def _rope_tables(positions):
    inv_freq = 1.0 / (ROPE_THETA ** (jnp.arange(0, ROPE_DIM, 2, dtype=F32) / ROPE_DIM))
    angles = positions.astype(F32).reshape(-1, 1) * inv_freq
    cos, sin = lax.optimization_barrier((jnp.cos(angles), jnp.sin(angles)))
    t = angles.shape[0]
    zeros = functools.partial(jnp.zeros, dtype=F32)
    c = jnp.concatenate([cos, cos, jnp.ones((t, LANES - ROPE_DIM), F32)], axis=1)
    sa = jnp.concatenate([-sin, zeros((t, LANES - ROPE_HALF))], axis=1)
    sb = jnp.concatenate([zeros((t, ROPE_HALF)), sin, zeros((t, LANES - ROPE_DIM))], axis=1)
    return c, sa, sb


def kernel(x, positions, norm1_g, w_in, lambda_q1, lambda_k1, lambda_q2, lambda_k2, subln_g, w_pool, pool_scale, w_out, norm2_g, w_gate, w_up, conv_w, conv_b, w_down, norm_f_g):
    batch, seq, d = x.shape
    depth = w_in.shape[0]
    pool_width = pool_scale.shape[1]
    attn_width = (w_in.shape[2] - pool_width) // 3
    rope_c, rope_sa, rope_sb = _rope_tables(positions)
    row = lambda v: v.reshape(1, -1)
    h = x.reshape(batch * seq, d)
    for l in range(depth):
        lambda_init = 0.8 - 0.6 * math.exp(-0.3 * l)
        qkv, u = _inproj(h, row(norm1_g[l]), w_in[l].astype(BF16), rope_c, rope_sa, rope_sb, attn_width)
        pool_w = w_pool[l]
        mlp_tile = _tile(w_gate.shape[2], MLP_TILE)
        attn, w_gate_b, w_up_b, w_down_b, w_out_b, w_pool_b = _attention(
            qkv, row(lambda_q1[l]), row(lambda_k1[l]), row(lambda_q2[l]), row(lambda_k2[l]), row(subln_g[l]),
            batch, seq, attn_width, lambda_init,
            cast_weights=(w_gate[l], w_up[l], w_down[l], w_out[l], pool_w.reshape(-1, pool_w.shape[-1])),
            cast_tiles=(mlp_tile, mlp_tile, None, None, None))
        pool = _pooling(u, w_pool_b.reshape(pool_w.shape), row(pool_scale[l]), batch, seq)
        h = _outproj(attn, pool, w_out_b, h)
        h = _mlp(h, row(norm2_g[l]), w_gate_b, w_up_b, conv_w[l], row(conv_b[l]), w_down_b, row(norm_f_g), seq,
                 final_norm=(l == depth - 1))
    return h.reshape(batch, seq, d)
```

```python
import functools
import math

import jax
import jax.numpy as jnp
from jax import lax
from jax.experimental import pallas as pl
from jax.experimental.pallas import tpu as pltpu

F32 = jnp.float32
BF16 = jnp.bfloat16

DIFF_HEAD_DIM = 128
V_HEAD_DIM = 2 * DIFF_HEAD_DIM
ROPE_DIM = DIFF_HEAD_DIM // 4
ROPE_HALF = ROPE_DIM // 2
ROPE_THETA = 500000.0
SUBLN_EPS = 1e-5
NORM_EPS = 1e-5
POOL_WINDOWS = (2, 4, 8, 16)
MAX_WINDOW = max(POOL_WINDOWS)
CONV_WIDTH = 3
MASK_VALUE = -1e30
LOG2_E = math.log2(math.e)

LANES = 128
MXU_WIDTH = 256
NORM_ROWS = 64
MLP_TILE = MXU_WIDTH
DOWN_SLAB = MXU_WIDTH
BF16_SUBLANES = 16
VMEM_LIMIT_BYTES = 58 * 1024 * 1024
HALO = BF16_SUBLANES

assert HALO >= MAX_WINDOW and HALO >= CONV_WIDTH - 1


def _tile(dim, pref):
    t = min(dim, pref)
    while dim % t:
        t -= 1
    return t


def _rms(x, g, eps):
    return x * lax.rsqrt(jnp.mean(x * x, axis=-1, keepdims=True) + eps) * g


def _rms_rows(src_ref, g_ref, dst_ref, dst_row0=0, copy_ref=None):
    def body(c, carry):
        r0 = pl.multiple_of(c * NORM_ROWS, NORM_ROWS)
        x = src_ref[pl.ds(r0, NORM_ROWS), :]
        if copy_ref is not None:
            copy_ref[pl.ds(r0, NORM_ROWS), :] = x
        dst_ref[pl.ds(dst_row0 + r0, NORM_ROWS), :] = _rms(x, g_ref[...], NORM_EPS).astype(dst_ref.dtype)
        return carry

    lax.fori_loop(0, src_ref.shape[0] // NORM_ROWS, body, 0)


def _params(*sem):
    return pltpu.CompilerParams(dimension_semantics=sem, vmem_limit_bytes=VMEM_LIMIT_BYTES)


def _inproj_kernel(x_ref, g_ref, wr_ref, wp_ref, c_ref, sa_ref, sb_ref, qk_ref, v_ref, u_ref, n_ref,
                   *, q_steps, scale):
    j = pl.program_id(1)

    @pl.when(j == 0)
    def _():
        _rms_rows(x_ref, g_ref, n_ref)

    width = wr_ref.shape[1]
    sub = min(width, MXU_WIDTH)

    def rope(t, mult):
        r = (t * c_ref[...] + pltpu.roll(t, LANES - ROPE_HALF, 1) * sa_ref[...]
             + pltpu.roll(t, ROPE_HALF, 1) * sb_ref[...])
        return r if mult is None else r * mult

    def project(mult, plain_ref):
        for c0 in range(0, width, sub):
            acc = jnp.dot(n_ref[...], wr_ref[:, c0:c0 + sub], preferred_element_type=F32)
            for k in range(0, sub, LANES):
                qk_ref[:, c0 + k:c0 + k + LANES] = rope(acc[:, k:k + LANES], mult).astype(BF16)
        for c0 in range(0, width, sub):
            acc = jnp.dot(n_ref[...], wp_ref[:, c0:c0 + sub], preferred_element_type=F32)
            plain_ref[:, c0:c0 + sub] = acc.astype(plain_ref.dtype)

    pl.when(j < q_steps)(lambda: project(scale, v_ref))
    pl.when(j >= q_steps)(lambda: project(None, u_ref))


def _inproj(h, g, w, rope_c, rope_sa, rope_sb, attn_width):
    t, d = h.shape
    n_out = w.shape[1]
    assert n_out == 4 * attn_width, "the pooling group is as wide as the attention group"
    tm = _tile(t, 512)
    width = _tile(attn_width, 512)
    q_steps = attn_width // width
    return pl.pallas_call(
        functools.partial(_inproj_kernel, q_steps=q_steps, scale=DIFF_HEAD_DIM ** -0.5 * LOG2_E),
        grid=(t // tm, 2 * q_steps),
        in_specs=[
            pl.BlockSpec((tm, d), lambda i, j: (i, 0)),
            pl.BlockSpec((1, d), lambda i, j: (0, 0)),
            pl.BlockSpec((d, width), lambda i, j: (0, j)),
            pl.BlockSpec((d, width), lambda i, j: (0, 2 * q_steps + j)),
            pl.BlockSpec((tm, LANES), lambda i, j: (i, 0)),
            pl.BlockSpec((tm, LANES), lambda i, j: (i, 0)),
            pl.BlockSpec((tm, LANES), lambda i, j: (i, 0)),
        ],
        out_specs=[
            pl.BlockSpec((tm, width), lambda i, j: (i, j)),
            pl.BlockSpec((tm, width), lambda i, j: (i, jnp.minimum(j, q_steps - 1))),
            pl.BlockSpec((tm, width), lambda i, j: (i, jnp.maximum(j - q_steps, 0))),
        ],
        out_shape=[
            jax.ShapeDtypeStruct((t, 2 * attn_width), BF16),
            jax.ShapeDtypeStruct((t, attn_width), BF16),
            jax.ShapeDtypeStruct((t, attn_width), F32),
        ],
        scratch_shapes=[pltpu.VMEM((tm, d), BF16)],
        compiler_params=_params("parallel", "arbitrary"),
        name="inproj",
    )(h, g, w, w, rope_c, rope_sa, rope_sb)


def _attn_kernel(*refs, tq, tk, lambda_init, cast_blocks):
    n_cast = len(cast_blocks)
    lq1_ref, lk1_ref, lq2_ref, lk2_ref, sg_ref, q_ref, k_ref, v_ref = refs[:8]
    w32_refs = refs[8:8 + n_cast]
    o_ref = refs[8 + n_cast]
    w16_refs = refs[9 + n_cast:9 + 2 * n_cast]
    vt_ref, acc_ref, sa_ref, sb_ref, m_ref, l_ref = refs[9 + 2 * n_cast:]
    qi = pl.program_id(2)
    dh = DIFF_HEAD_DIM

    step_id = (pl.program_id(0) * pl.num_programs(1) + pl.program_id(1)) * pl.num_programs(2) + qi
    for w32_ref, w16_ref, n_blocks in zip(w32_refs, w16_refs, cast_blocks):
        def cast(w32_ref=w32_ref, w16_ref=w16_ref):
            if len(w16_ref.shape) == 3:
                tile = w16_ref.shape[2]
                for t in range(w16_ref.shape[0]):
                    w16_ref[t] = w32_ref[:, t * tile:(t + 1) * tile].astype(BF16)
            else:
                w16_ref[...] = w32_ref[...].astype(BF16)
        pl.when(step_id < n_blocks)(cast)

    @pl.when(qi == 0)
    def _():
        for j in range(vt_ref.shape[0]):
            vt_ref[j] = v_ref[j * tk:(j + 1) * tk, :].T

    acc_ref[...] = jnp.zeros_like(acc_ref)
    for c in range(2):
        m_ref[c] = jnp.full((1, tq), MASK_VALUE, F32)
        l_ref[c] = jnp.zeros((1, tq), F32)

    def scores(j, dst_ref, q0=0):
        kv0 = pl.multiple_of(j * tk, tk)
        for c in range(2):
            dst_ref[c, :, q0:] = lax.dot_general(
                k_ref[pl.ds(kv0, tk), c * dh:(c + 1) * dh], q_ref[q0:, c * dh:(c + 1) * dh],
                (((1,), (1,)), ((), ())), preferred_element_type=F32)

    def softmax_pv(j, src_ref, diag_offset=None, q0=0):
        vt = vt_ref[j]
        for c in range(2):
            st = src_ref[c, :, q0:]
            if diag_offset is not None:
                key = lax.broadcasted_iota(jnp.int32, st.shape, 0) + diag_offset
                qry = lax.broadcasted_iota(jnp.int32, st.shape, 1) + q0
                st = jnp.where(key <= qry, st, MASK_VALUE)
            m_prev = m_ref[c, :, q0:]
            m_new = jnp.maximum(m_prev, jnp.max(st, axis=0, keepdims=True))
            alpha = jnp.exp2(m_prev - m_new)
            pt = jnp.exp2(st - m_new)
            m_ref[c, :, q0:] = m_new
            l_ref[c, :, q0:] = alpha * l_ref[c, :, q0:] + jnp.sum(pt, axis=0, keepdims=True)
            acc_ref[c, :, q0:] = (alpha * acc_ref[c, :, q0:]
                                  + jnp.dot(vt, pt.astype(BF16), preferred_element_type=F32))

    def step(j, src_ref, dst_ref):
        softmax_pv(j, src_ref)
        scores(j + 1, dst_ref)

    def pair(jj, carry):
        step(2 * jj, sa_ref, sb_ref)
        step(2 * jj + 1, sb_ref, sa_ref)
        return carry

    scores(0, sa_ref)
    lax.fori_loop(0, qi, pair, 0)
    softmax_pv(2 * qi, sa_ref, diag_offset=0)
    scores(2 * qi + 1, sb_ref, q0=tk)
    softmax_pv(2 * qi + 1, sb_ref, diag_offset=tk, q0=tk)

    lam = (jnp.exp(jnp.sum(lq1_ref[...] * lk1_ref[...], axis=-1, keepdims=True))
           - jnp.exp(jnp.sum(lq2_ref[...] * lk2_ref[...], axis=-1, keepdims=True)) + lambda_init)
    ot = acc_ref[0] * (1.0 / l_ref[0]) - lam * (acc_ref[1] * (1.0 / l_ref[1]))
    yt = ot * lax.rsqrt(jnp.mean(ot * ot, axis=0, keepdims=True) + SUBLN_EPS) * sg_ref[...]
    o_ref[...] = (yt * (1.0 - lambda_init)).T.astype(BF16)


def _cast_rows(rows, n_steps):
    r = BF16_SUBLANES
    while r < rows and (rows % r or rows // r > n_steps):
        r += BF16_SUBLANES
    return r


def _attention(qk, v, lq1, lk1, lq2, lk2, subln_g, batch, seq, attn_width, lambda_init, cast_weights, cast_tiles):
    t = qk.shape[0]
    heads = attn_width // V_HEAD_DIM
    tq = _tile(seq, 1024)
    tk = tq // 2
    assert tq == 2 * tk and seq % tk == 0
    nq = seq // tq
    vec = pl.BlockSpec((1, DIFF_HEAD_DIM), lambda b, h, i: (0, 0))
    scores = pltpu.VMEM((2, tk, tq), F32)
    stat = pltpu.VMEM((2, 1, tq), F32)
    n_steps = batch * heads * nq
    cast_rows = [_cast_rows(w.shape[0], n_steps) for w in cast_weights]
    cast_blocks = tuple(w.shape[0] // r for w, r in zip(cast_weights, cast_rows))
    assert all(n <= n_steps for n in cast_blocks)

    def row_block(n_blocks):
        return lambda b, h, i: jnp.minimum((b * heads + h) * nq + i, n_blocks - 1)

    def in_spec(w, r, n_blocks):
        block = row_block(n_blocks)
        return pl.BlockSpec((r, w.shape[1]), lambda b, h, i: (block(b, h, i), 0))

    def out_spec(w, r, n_blocks, tile):
        block = row_block(n_blocks)
        if tile is None:
            return pl.BlockSpec((r, w.shape[1]), lambda b, h, i: (block(b, h, i), 0))
        return pl.BlockSpec((w.shape[1] // tile, r, tile), lambda b, h, i: (0, block(b, h, i), 0))

    def out_shape(w, tile):
        shape = w.shape if tile is None else (w.shape[1] // tile, w.shape[0], tile)
        return jax.ShapeDtypeStruct(shape, BF16)

    cast_in = [in_spec(w, r, n) for w, r, n in zip(cast_weights, cast_rows, cast_blocks)]
    cast_out = [out_spec(w, r, n, tl) for w, r, n, tl in zip(cast_weights, cast_rows, cast_blocks, cast_tiles)]
    return pl.pallas_call(
        functools.partial(_attn_kernel, tq=tq, tk=tk, lambda_init=lambda_init, cast_blocks=cast_blocks),
        grid=(batch, heads, nq),
        in_specs=[
            vec, vec, vec, vec,
            pl.BlockSpec((V_HEAD_DIM, 1), lambda b, h, i: (0, 0)),
            pl.BlockSpec((tq, V_HEAD_DIM), lambda b, h, i: (b * nq + i, h)),
            pl.BlockSpec((seq, V_HEAD_DIM), lambda b, h, i: (b, heads + h)),
            pl.BlockSpec((seq, V_HEAD_DIM), lambda b, h, i: (b, h)),
        ] + cast_in,
        out_specs=[pl.BlockSpec((tq, V_HEAD_DIM), lambda b, h, i: (b * nq + i, h))] + cast_out,
        out_shape=[jax.ShapeDtypeStruct((t, attn_width), BF16)]
        + [out_shape(w, tl) for w, tl in zip(cast_weights, cast_tiles)],
        scratch_shapes=[
            pltpu.VMEM((seq // tk, V_HEAD_DIM, tk), BF16),
            pltpu.VMEM((2, V_HEAD_DIM, tq), F32),
            scores, scores,
            stat, stat,
        ],
        compiler_params=_params("arbitrary", "arbitrary", "arbitrary"),
        name="diff_attention",
    )(lq1, lk1, lq2, lk2, subln_g.reshape(V_HEAD_DIM, 1), qk, qk, v, *cast_weights)


def _pool_kernel(u_ref, w_ref, sc_ref, o_ref, *, seq, rows):
    g = pl.program_id(0)
    cols = u_ref.shape[1]

    def chunk(r0, ext, window):
        s, span = ext, 1
        while span < window:
            s = s + pltpu.roll(s, span, 0)
            span *= 2
        u = ext[HALO:]
        pos = r0 + lax.broadcasted_iota(jnp.int32, (rows, 1), 0)
        inv_count = 1.0 / jnp.minimum(pos + 1, window).astype(F32)
        pooled = s[HALO:] * inv_count - u
        y = jnp.dot(pooled.astype(BF16), w_ref[...], preferred_element_type=F32) * sc_ref[...]
        o_ref[pl.ds(r0, rows), :] = y.astype(BF16)

    def run(window):
        chunk(0, jnp.concatenate([jnp.zeros((HALO, cols), F32), u_ref[pl.ds(0, rows), :]], axis=0), window)

        def body(c, carry):
            r0 = pl.multiple_of(c * rows, rows)
            chunk(r0, u_ref[pl.ds(r0 - HALO, rows + HALO), :], window)
            return carry

        lax.fori_loop(1, seq // rows, body, 0)

    for gi, window in enumerate(POOL_WINDOWS):
        pl.when(g == gi)(functools.partial(run, window))


def _pooling(u, w_pool, pool_scale, batch, seq):
    t, width = u.shape
    groups = len(POOL_WINDOWS)
    cols = width // groups
    rows = _tile(seq, 512)
    return pl.pallas_call(
        functools.partial(_pool_kernel, seq=seq, rows=rows),
        grid=(groups, batch),
        in_specs=[
            pl.BlockSpec((seq, cols), lambda g, b: (b, g)),
            pl.BlockSpec((None, cols, cols), lambda g, b: (g, 0, 0)),
            pl.BlockSpec((1, cols), lambda g, b: (0, g)),
        ],
        out_specs=pl.BlockSpec((seq, cols), lambda g, b: (b, g)),
        out_shape=jax.ShapeDtypeStruct((t, width), BF16),
        compiler_params=_params("parallel", "parallel"),
        name="pool",
    )(u, w_pool, pool_scale)


def _outproj_kernel(a_ref, p_ref, wa_ref, wp_ref, x_ref, h_ref):
    acc = jnp.dot(a_ref[...], wa_ref[...], preferred_element_type=F32)
    acc = acc + jnp.dot(p_ref[...], wp_ref[...], preferred_element_type=F32)
    h_ref[...] = x_ref[...] + acc


def _outproj(attn, pool, w_out, x):
    t, d = x.shape
    aw, pw = attn.shape[1], pool.shape[1]
    assert aw == pw, "the two mixer groups are equally wide"
    tm = _tile(t, 1024)
    tn = _tile(d, 1024)
    return pl.pallas_call(
        _outproj_kernel,
        grid=(t // tm, d // tn),
        in_specs=[
            pl.BlockSpec((tm, aw), lambda i, j: (i, 0)),
            pl.BlockSpec((tm, pw), lambda i, j: (i, 0)),
            pl.BlockSpec((aw, tn), lambda i, j: (0, j)),
            pl.BlockSpec((pw, tn), lambda i, j: (1, j)),
            pl.BlockSpec((tm, tn), lambda i, j: (i, j)),
        ],
        out_specs=pl.BlockSpec((tm, tn), lambda i, j: (i, j)),
        out_shape=jax.ShapeDtypeStruct((t, d), F32),
        compiler_params=_params("parallel", "parallel"),
        name="outproj",
    )(attn, pool, w_out, w_out, x)


def _mlp_kernel(h_ref, halo_ref, g2_ref, wg_ref, wu_ref, cw_ref, cb_ref, wd_ref, gf_ref, o_ref, n_ref,
                *, tm, tiles_per_seq, final_norm):
    i = pl.program_id(0)
    j = pl.program_id(1)

    @pl.when(j == 0)
    def _():
        keep = (i % tiles_per_seq != 0).astype(F32)
        n_ref[:HALO, :] = (_rms(halo_ref[...], g2_ref[...], NORM_EPS) * keep).astype(BF16)
        _rms_rows(h_ref, g2_ref, n_ref, dst_row0=HALO, copy_ref=o_ref)

    gate = jnp.dot(n_ref[...], wg_ref[...], preferred_element_type=F32)
    up = jnp.dot(n_ref[HALO:, :], wu_ref[...], preferred_element_type=F32)
    cw = 0.5 * cw_ref[...]
    half = 0.5 * cb_ref[...]
    for tap in range(CONV_WIDTH):
        lag = CONV_WIDTH - 1 - tap
        lagged = pltpu.roll(gate, lag, 0) if lag else gate
        half = half + cw[tap:tap + 1, :] * lagged[HALO:]
    act = ((half + half * jnp.tanh(half)) * up).astype(BF16)
    for c0 in range(0, o_ref.shape[1], DOWN_SLAB):
        cols = slice(c0, c0 + DOWN_SLAB)
        o_ref[:, cols] += jnp.dot(act, wd_ref[:, cols], preferred_element_type=F32)

    if final_norm:
        @pl.when(j == pl.num_programs(1) - 1)
        def _():
            _rms_rows(o_ref, gf_ref, o_ref)


def _mlp(h, g2, w_gate, w_up, conv_w, conv_b, w_down, gf, seq, final_norm):
    t, d = h.shape
    n_tiles, _, tf = w_gate.shape
    f = n_tiles * tf
    tm = _tile(seq, 512)
    assert tm % HALO == 0
    halo_blocks = tm // HALO
    return pl.pallas_call(
        functools.partial(_mlp_kernel, tm=tm, tiles_per_seq=seq // tm, final_norm=final_norm),
        grid=(t // tm, f // tf),
        in_specs=[
            pl.BlockSpec((tm, d), lambda i, j: (i, 0)),
            pl.BlockSpec((HALO, d), lambda i, j: (jnp.maximum(i * halo_blocks - 1, 0), 0)),
            pl.BlockSpec((1, d), lambda i, j: (0, 0)),
            pl.BlockSpec((None, d, tf), lambda i, j: (j, 0, 0)),
            pl.BlockSpec((None, d, tf), lambda i, j: (j, 0, 0)),
            pl.BlockSpec((CONV_WIDTH, tf), lambda i, j: (0, j)),
            pl.BlockSpec((1, tf), lambda i, j: (0, j)),
            pl.BlockSpec((tf, d), lambda i, j: (j, 0)),
            pl.BlockSpec((1, d), lambda i, j: (0, 0)),
        ],
        out_specs=pl.BlockSpec((tm, d), lambda i, j: (i, 0)),
        out_shape=jax.ShapeDtypeStruct((t, d), F32),
        scratch_shapes=[pltpu.VMEM((HALO + tm, d), BF16)],
        compiler_params=_params("parallel", "arbitrary"),
        name="conv_mlp",
    )(h, h, g2, w_gate, w_up, conv_w, conv_b, w_down, gf)


def _rope_tables(positions):
    inv_freq = 1.0 / (ROPE_THETA ** (jnp.arange(0, ROPE_DIM, 2, dtype=F32) / ROPE_DIM))
    angles = positions.astype(F32).reshape(-1, 1) * inv_freq
    cos, sin = lax.optimization_barrier((jnp.cos(angles), jnp.sin(angles)))
    t = angles.shape[0]
    zeros = functools.partial(jnp.zeros, dtype=F32)
    c = jnp.concatenate([cos, cos, jnp.ones((t, LANES - ROPE_DIM), F32)], axis=1)
    sa = jnp.concatenate([-sin, zeros((t, LANES - ROPE_HALF))], axis=1)
    sb = jnp.concatenate([zeros((t, ROPE_HALF)), sin, zeros((t, LANES - ROPE_DIM))], axis=1)
    return c, sa, sb


def kernel(x, positions, norm1_g, w_in, lambda_q1, lambda_k1, lambda_q2, lambda_k2, subln_g, w_pool, pool_scale, w_out, norm2_g, w_gate, w_up, conv_w, conv_b, w_down, norm_f_g):
    batch, seq, d = x.shape
    depth = w_in.shape[0]
    pool_width = pool_scale.shape[1]
    attn_width = (w_in.shape[2] - pool_width) // 3
    rope_c, rope_sa, rope_sb = _rope_tables(positions)
    row = lambda v: v.reshape(1, -1)
    h = x.reshape(batch * seq, d)
    for l in range(depth):
        lambda_init = 0.8 - 0.6 * math.exp(-0.3 * l)
        qk, v, u = _inproj(h, row(norm1_g[l]), w_in[l].astype(BF16), rope_c, rope_sa, rope_sb, attn_width)
        pool_w = w_pool[l]
        mlp_tile = _tile(w_gate.shape[2], MLP_TILE)
        attn, w_gate_b, w_up_b, w_down_b, w_out_b, w_pool_b = _attention(
            qk, v, row(lambda_q1[l]), row(lambda_k1[l]), row(lambda_q2[l]), row(lambda_k2[l]), row(subln_g[l]),
            batch, seq, attn_width, lambda_init,
            cast_weights=(w_gate[l], w_up[l], w_down[l], w_out[l], pool_w.reshape(-1, pool_w.shape[-1])),
            cast_tiles=(mlp_tile, mlp_tile, None, None, None))
        pool = _pooling(u, w_pool_b.reshape(pool_w.shape), row(pool_scale[l]), batch, seq)
        h = _outproj(attn, pool, w_out_b, h)
        h = _mlp(h, row(norm2_g[l]), w_gate_b, w_up_b, conv_w[l], row(conv_b[l]), w_down_b, row(norm_f_g), seq,
                 final_norm=(l == depth - 1))
    return h.reshape(batch, seq, d)
```

```python
import functools
import math

import jax
import jax.numpy as jnp
from jax import lax
from jax.experimental import pallas as pl
from jax.experimental.pallas import tpu as pltpu

F32 = jnp.float32
BF16 = jnp.bfloat16

DIFF_HEAD_DIM = 128
V_HEAD_DIM = 2 * DIFF_HEAD_DIM
ROPE_DIM = DIFF_HEAD_DIM // 4
ROPE_HALF = ROPE_DIM // 2
ROPE_THETA = 500000.0
SUBLN_EPS = 1e-5
NORM_EPS = 1e-5
POOL_WINDOWS = (2, 4, 8, 16)
MAX_WINDOW = max(POOL_WINDOWS)
CONV_WIDTH = 3
MASK_VALUE = -1e30
LOG2_E = math.log2(math.e)

LANES = 128
MXU_WIDTH = 256
NORM_ROWS = 64
MLP_TILE = MXU_WIDTH
DOWN_SLAB = MXU_WIDTH
BF16_SUBLANES = 16
VMEM_LIMIT_BYTES = 58 * 1024 * 1024
HALO = BF16_SUBLANES

assert HALO >= MAX_WINDOW and HALO >= CONV_WIDTH - 1


def _tile(dim, pref):
    t = min(dim, pref)
    while dim % t:
        t -= 1
    return t


def _rms(x, g, eps):
    return x * lax.rsqrt(jnp.mean(x * x, axis=-1, keepdims=True) + eps) * g


def _rms_rows(src_ref, g_ref, dst_ref, dst_row0=0, copy_ref=None):
    assert src_ref.shape[0] % NORM_ROWS == 0

    def body(c, carry):
        r0 = pl.multiple_of(c * NORM_ROWS, NORM_ROWS)
        x = src_ref[pl.ds(r0, NORM_ROWS), :]
        if copy_ref is not None:
            copy_ref[pl.ds(r0, NORM_ROWS), :] = x
        dst_ref[pl.ds(dst_row0 + r0, NORM_ROWS), :] = _rms(x, g_ref[...], NORM_EPS).astype(dst_ref.dtype)
        return carry

    lax.fori_loop(0, src_ref.shape[0] // NORM_ROWS, body, 0)


def _params(*sem):
    return pltpu.CompilerParams(dimension_semantics=sem, vmem_limit_bytes=VMEM_LIMIT_BYTES)


def _inproj_kernel(x_ref, g_ref, wr_ref, wp_ref, c_ref, sa_ref, sb_ref, qk_ref, v_ref, u_ref, n_ref,
                   *, q_steps, scale):
    j = pl.program_id(1)

    @pl.when(j == 0)
    def _():
        _rms_rows(x_ref, g_ref, n_ref)

    width = wr_ref.shape[1]
    sub = min(width, MXU_WIDTH)

    def rope(t, mult):
        r = (t * c_ref[...] + pltpu.roll(t, LANES - ROPE_HALF, 1) * sa_ref[...]
             + pltpu.roll(t, ROPE_HALF, 1) * sb_ref[...])
        return r if mult is None else r * mult

    def project(mult, plain_ref):
        for c0 in range(0, width, sub):
            acc = jnp.dot(n_ref[...], wr_ref[:, c0:c0 + sub], preferred_element_type=F32)
            for k in range(0, sub, LANES):
                qk_ref[:, c0 + k:c0 + k + LANES] = rope(acc[:, k:k + LANES], mult).astype(BF16)
        for c0 in range(0, width, sub):
            acc = jnp.dot(n_ref[...], wp_ref[:, c0:c0 + sub], preferred_element_type=F32)
            plain_ref[:, c0:c0 + sub] = acc.astype(plain_ref.dtype)

    pl.when(j < q_steps)(lambda: project(scale, v_ref))
    pl.when(j >= q_steps)(lambda: project(None, u_ref))


def _inproj(h, g, w, rope_c, rope_sa, rope_sb, attn_width):
    t, d = h.shape
    n_out = w.shape[1]
    assert n_out == 4 * attn_width, "the pooling group is as wide as the attention group"
    tm = _tile(t, 512)
    width = _tile(attn_width, 512)
    q_steps = attn_width // width
    return pl.pallas_call(
        functools.partial(_inproj_kernel, q_steps=q_steps, scale=DIFF_HEAD_DIM ** -0.5 * LOG2_E),
        grid=(t // tm, 2 * q_steps),
        in_specs=[
            pl.BlockSpec((tm, d), lambda i, j: (i, 0)),
            pl.BlockSpec((1, d), lambda i, j: (0, 0)),
            pl.BlockSpec((d, width), lambda i, j: (0, j)),
            pl.BlockSpec((d, width), lambda i, j: (0, 2 * q_steps + j)),
            pl.BlockSpec((tm, LANES), lambda i, j: (i, 0)),
            pl.BlockSpec((tm, LANES), lambda i, j: (i, 0)),
            pl.BlockSpec((tm, LANES), lambda i, j: (i, 0)),
        ],
        out_specs=[
            pl.BlockSpec((tm, width), lambda i, j: (i, j)),
            pl.BlockSpec((tm, width), lambda i, j: (i, jnp.minimum(j, q_steps - 1))),
            pl.BlockSpec((tm, width), lambda i, j: (i, jnp.maximum(j - q_steps, 0))),
        ],
        out_shape=[
            jax.ShapeDtypeStruct((t, 2 * attn_width), BF16),
            jax.ShapeDtypeStruct((t, attn_width), BF16),
            jax.ShapeDtypeStruct((t, attn_width), F32),
        ],
        scratch_shapes=[pltpu.VMEM((tm, d), BF16)],
        compiler_params=_params("parallel", "arbitrary"),
        name="inproj",
    )(h, g, w, w, rope_c, rope_sa, rope_sb)


def _attn_kernel(*refs, tq, tk, lambda_init, cast_blocks):
    n_cast = len(cast_blocks)
    lq1_ref, lk1_ref, lq2_ref, lk2_ref, sg_ref, q_ref, k_ref, v_ref = refs[:8]
    w32_refs = refs[8:8 + n_cast]
    o_ref = refs[8 + n_cast]
    w16_refs = refs[9 + n_cast:9 + 2 * n_cast]
    vt_ref, acc_ref, sa_ref, sb_ref, m_ref, l_ref = refs[9 + 2 * n_cast:]
    qi = pl.program_id(2)
    dh = DIFF_HEAD_DIM

    step_id = (pl.program_id(0) * pl.num_programs(1) + pl.program_id(1)) * pl.num_programs(2) + qi
    for w32_ref, w16_ref, n_blocks in zip(w32_refs, w16_refs, cast_blocks):
        def cast(w32_ref=w32_ref, w16_ref=w16_ref):
            if len(w16_ref.shape) == 3:
                tile = w16_ref.shape[2]
                for t in range(w16_ref.shape[0]):
                    w16_ref[t] = w32_ref[:, t * tile:(t + 1) * tile].astype(BF16)
            else:
                w16_ref[...] = w32_ref[...].astype(BF16)
        pl.when(step_id < n_blocks)(cast)

    @pl.when(qi == 0)
    def _():
        for j in range(vt_ref.shape[0]):
            vt_ref[j] = v_ref[j * tk:(j + 1) * tk, :].T

    acc_ref[...] = jnp.zeros_like(acc_ref)
    for c in range(2):
        m_ref[c] = jnp.full((1, tq), MASK_VALUE, F32)
        l_ref[c] = jnp.zeros((1, tq), F32)

    def scores(j, dst_ref, q0=0):
        kv0 = pl.multiple_of(j * tk, tk)
        for c in range(2):
            dst_ref[c, :, q0:] = lax.dot_general(
                k_ref[pl.ds(kv0, tk), c * dh:(c + 1) * dh], q_ref[q0:, c * dh:(c + 1) * dh],
                (((1,), (1,)), ((), ())), preferred_element_type=F32)

    def softmax_pv(j, src_ref, diag_offset=None, q0=0):
        vt = vt_ref[j]
        for c in range(2):
            st = src_ref[c, :, q0:]
            if diag_offset is not None:
                key = lax.broadcasted_iota(jnp.int32, st.shape, 0) + diag_offset
                qry = lax.broadcasted_iota(jnp.int32, st.shape, 1) + q0
                st = jnp.where(key <= qry, st, MASK_VALUE)
            m_prev = m_ref[c, :, q0:]
            m_new = jnp.maximum(m_prev, jnp.max(st, axis=0, keepdims=True))
            alpha = jnp.exp2(m_prev - m_new)
            pt = jnp.exp2(st - m_new)
            m_ref[c, :, q0:] = m_new
            l_ref[c, :, q0:] = alpha * l_ref[c, :, q0:] + jnp.sum(pt, axis=0, keepdims=True)
            acc_ref[c, :, q0:] = (alpha * acc_ref[c, :, q0:]
                                  + jnp.dot(vt, pt.astype(BF16), preferred_element_type=F32))

    def step(j, src_ref, dst_ref):
        softmax_pv(j, src_ref)
        scores(j + 1, dst_ref)

    def pair(jj, carry):
        step(2 * jj, sa_ref, sb_ref)
        step(2 * jj + 1, sb_ref, sa_ref)
        return carry

    scores(0, sa_ref)
    lax.fori_loop(0, qi, pair, 0)
    softmax_pv(2 * qi, sa_ref, diag_offset=0)
    scores(2 * qi + 1, sb_ref, q0=tk)
    softmax_pv(2 * qi + 1, sb_ref, diag_offset=tk, q0=tk)

    lam = (jnp.exp(jnp.sum(lq1_ref[...] * lk1_ref[...], axis=-1, keepdims=True))
           - jnp.exp(jnp.sum(lq2_ref[...] * lk2_ref[...], axis=-1, keepdims=True)) + lambda_init)
    ot = acc_ref[0] * (1.0 / l_ref[0]) - lam * (acc_ref[1] * (1.0 / l_ref[1]))
    yt = ot * lax.rsqrt(jnp.mean(ot * ot, axis=0, keepdims=True) + SUBLN_EPS) * sg_ref[...]
    o_ref[...] = (yt * (1.0 - lambda_init)).T.astype(BF16)


def _cast_rows(rows, n_steps):
    r = BF16_SUBLANES
    while r < rows and (rows % r or rows // r > n_steps):
        r += BF16_SUBLANES
    return r


def _attention(qk, v, lq1, lk1, lq2, lk2, subln_g, batch, seq, attn_width, lambda_init, cast_weights, cast_tiles):
    t = qk.shape[0]
    heads = attn_width // V_HEAD_DIM
    tq = _tile(seq, 1024)
    tk = tq // 2
    assert tq == 2 * tk and seq % tk == 0
    nq = seq // tq
    vec = pl.BlockSpec((1, DIFF_HEAD_DIM), lambda b, h, i: (0, 0))
    scores = pltpu.VMEM((2, tk, tq), F32)
    stat = pltpu.VMEM((2, 1, tq), F32)
    n_steps = batch * heads * nq
    cast_rows = [_cast_rows(w.shape[0], n_steps) for w in cast_weights]
    cast_blocks = tuple(w.shape[0] // r for w, r in zip(cast_weights, cast_rows))
    assert all(n <= n_steps for n in cast_blocks)

    def row_block(n_blocks):
        return lambda b, h, i: jnp.minimum((b * heads + h) * nq + i, n_blocks - 1)

    def in_spec(w, r, n_blocks):
        block = row_block(n_blocks)
        return pl.BlockSpec((r, w.shape[1]), lambda b, h, i: (block(b, h, i), 0))

    def out_spec(w, r, n_blocks, tile):
        block = row_block(n_blocks)
        if tile is None:
            return pl.BlockSpec((r, w.shape[1]), lambda b, h, i: (block(b, h, i), 0))
        return pl.BlockSpec((w.shape[1] // tile, r, tile), lambda b, h, i: (0, block(b, h, i), 0))

    def out_shape(w, tile):
        shape = w.shape if tile is None else (w.shape[1] // tile, w.shape[0], tile)
        return jax.ShapeDtypeStruct(shape, BF16)

    cast_in = [in_spec(w, r, n) for w, r, n in zip(cast_weights, cast_rows, cast_blocks)]
    cast_out = [out_spec(w, r, n, tl) for w, r, n, tl in zip(cast_weights, cast_rows, cast_blocks, cast_tiles)]
    return pl.pallas_call(
        functools.partial(_attn_kernel, tq=tq, tk=tk, lambda_init=lambda_init, cast_blocks=cast_blocks),
        grid=(batch, heads, nq),
        in_specs=[
            vec, vec, vec, vec,
            pl.BlockSpec((V_HEAD_DIM, 1), lambda b, h, i: (0, 0)),
            pl.BlockSpec((tq, V_HEAD_DIM), lambda b, h, i: (b * nq + i, h)),
            pl.BlockSpec((seq, V_HEAD_DIM), lambda b, h, i: (b, heads + h)),
            pl.BlockSpec((seq, V_HEAD_DIM), lambda b, h, i: (b, h)),
        ] + cast_in,
        out_specs=[pl.BlockSpec((tq, V_HEAD_DIM), lambda b, h, i: (b * nq + i, h))] + cast_out,
        out_shape=[jax.ShapeDtypeStruct((t, attn_width), BF16)]
        + [out_shape(w, tl) for w, tl in zip(cast_weights, cast_tiles)],
        scratch_shapes=[
            pltpu.VMEM((seq // tk, V_HEAD_DIM, tk), BF16),
            pltpu.VMEM((2, V_HEAD_DIM, tq), F32),
            scores, scores,
            stat, stat,
        ],
        compiler_params=_params("arbitrary", "arbitrary", "arbitrary"),
        name="diff_attention",
    )(lq1, lk1, lq2, lk2, subln_g.reshape(V_HEAD_DIM, 1), qk, qk, v, *cast_weights)


def _pool_kernel(u_ref, w_ref, sc_ref, o_ref, *, seq, rows):
    g = pl.program_id(0)
    cols = u_ref.shape[1]

    def chunk(r0, ext, window):
        s, span = ext, 1
        while span < window:
            s = s + pltpu.roll(s, span, 0)
            span *= 2
        u = ext[HALO:]
        pos = r0 + lax.broadcasted_iota(jnp.int32, (rows, 1), 0)
        inv_count = 1.0 / jnp.minimum(pos + 1, window).astype(F32)
        pooled = s[HALO:] * inv_count - u
        y = jnp.dot(pooled.astype(BF16), w_ref[...], preferred_element_type=F32) * sc_ref[...]
        o_ref[pl.ds(r0, rows), :] = y.astype(BF16)

    def run(window):
        chunk(0, jnp.concatenate([jnp.zeros((HALO, cols), F32), u_ref[pl.ds(0, rows), :]], axis=0), window)

        def body(c, carry):
            r0 = pl.multiple_of(c * rows, rows)
            chunk(r0, u_ref[pl.ds(r0 - HALO, rows + HALO), :], window)
            return carry

        lax.fori_loop(1, seq // rows, body, 0)

    for gi, window in enumerate(POOL_WINDOWS):
        pl.when(g == gi)(functools.partial(run, window))


def _pooling(u, w_pool, pool_scale, batch, seq):
    t, width = u.shape
    groups = len(POOL_WINDOWS)
    cols = width // groups
    rows = _tile(seq, 512)
    return pl.pallas_call(
        functools.partial(_pool_kernel, seq=seq, rows=rows),
        grid=(groups, batch),
        in_specs=[
            pl.BlockSpec((seq, cols), lambda g, b: (b, g)),
            pl.BlockSpec((None, cols, cols), lambda g, b: (g, 0, 0)),
            pl.BlockSpec((1, cols), lambda g, b: (0, g)),
        ],
        out_specs=pl.BlockSpec((seq, cols), lambda g, b: (b, g)),
        out_shape=jax.ShapeDtypeStruct((t, width), BF16),
        compiler_params=_params("parallel", "parallel"),
        name="pool",
    )(u, w_pool, pool_scale)


def _outproj_kernel(a_ref, p_ref, wa_ref, wp_ref, x_ref, h_ref):
    acc = jnp.dot(a_ref[...], wa_ref[...], preferred_element_type=F32)
    acc = acc + jnp.dot(p_ref[...], wp_ref[...], preferred_element_type=F32)
    h_ref[...] = x_ref[...] + acc


def _outproj(attn, pool, w_out, x):
    t, d = x.shape
    aw, pw = attn.shape[1], pool.shape[1]
    assert aw == pw, "the two mixer groups are equally wide"
    tm = _tile(t, 1024)
    tn = _tile(d, 1024)
    return pl.pallas_call(
        _outproj_kernel,
        grid=(t // tm, d // tn),
        in_specs=[
            pl.BlockSpec((tm, aw), lambda i, j: (i, 0)),
            pl.BlockSpec((tm, pw), lambda i, j: (i, 0)),
            pl.BlockSpec((aw, tn), lambda i, j: (0, j)),
            pl.BlockSpec((pw, tn), lambda i, j: (1, j)),
            pl.BlockSpec((tm, tn), lambda i, j: (i, j)),
        ],
        out_specs=pl.BlockSpec((tm, tn), lambda i, j: (i, j)),
        out_shape=jax.ShapeDtypeStruct((t, d), F32),
        compiler_params=_params("parallel", "parallel"),
        name="outproj",
    )(attn, pool, w_out, w_out, x)


def _mlp_kernel(h_ref, halo_ref, g2_ref, wg_ref, wu_ref, cw_ref, cb_ref, wd_ref, gf_ref, o_ref, n_ref,
                *, tm, tiles_per_seq, final_norm):
    i = pl.program_id(0)
    j = pl.program_id(1)

    @pl.when(j == 0)
    def _():
        keep = (i % tiles_per_seq != 0).astype(F32)
        n_ref[:HALO, :] = (_rms(halo_ref[...], g2_ref[...], NORM_EPS) * keep).astype(BF16)
        _rms_rows(h_ref, g2_ref, n_ref, dst_row0=HALO, copy_ref=o_ref)

    gate = jnp.dot(n_ref[...], wg_ref[...], preferred_element_type=F32)
    up = jnp.dot(n_ref[HALO:, :], wu_ref[...], preferred_element_type=F32)
    cw = 0.5 * cw_ref[...]
    half = 0.5 * cb_ref[...]
    for tap in range(CONV_WIDTH):
        lag = CONV_WIDTH - 1 - tap
        lagged = pltpu.roll(gate, lag, 0) if lag else gate
        half = half + cw[tap:tap + 1, :] * lagged[HALO:]
    act = ((half + half * jnp.tanh(half)) * up).astype(BF16)
    for c0 in range(0, o_ref.shape[1], DOWN_SLAB):
        cols = slice(c0, c0 + DOWN_SLAB)
        o_ref[:, cols] += jnp.dot(act, wd_ref[:, cols], preferred_element_type=F32)

    if final_norm:
        @pl.when(j == pl.num_programs(1) - 1)
        def _():
            _rms_rows(o_ref, gf_ref, o_ref)


def _mlp(h, g2, w_gate, w_up, conv_w, conv_b, w_down, gf, seq, final_norm):
    t, d = h.shape
    n_tiles, _, tf = w_gate.shape
    f = n_tiles * tf
    tm = _tile(seq, 512)
    assert tm % HALO == 0
    halo_blocks = tm // HALO
    return pl.pallas_call(
        functools.partial(_mlp_kernel, tm=tm, tiles_per_seq=seq // tm, final_norm=final_norm),
        grid=(t // tm, f // tf),
        in_specs=[
            pl.BlockSpec((tm, d), lambda i, j: (i, 0)),
            pl.BlockSpec((HALO, d), lambda i, j: (jnp.maximum(i * halo_blocks - 1, 0), 0)),
            pl.BlockSpec((1, d), lambda i, j: (0, 0)),
            pl.BlockSpec((None, d, tf), lambda i, j: (j, 0, 0)),
            pl.BlockSpec((None, d, tf), lambda i, j: (j, 0, 0)),
            pl.BlockSpec((CONV_WIDTH, tf), lambda i, j: (0, j)),
            pl.BlockSpec((1, tf), lambda i, j: (0, j)),
            pl.BlockSpec((tf, d), lambda i, j: (j, 0)),
            pl.BlockSpec((1, d), lambda i, j: (0, 0)),
        ],
        out_specs=pl.BlockSpec((tm, d), lambda i, j: (i, 0)),
        out_shape=jax.ShapeDtypeStruct((t, d), F32),
        scratch_shapes=[pltpu.VMEM((HALO + tm, d), BF16)],
        compiler_params=_params("parallel", "arbitrary"),
        name="conv_mlp",
    )(h, h, g2, w_gate, w_up, conv_w, conv_b, w_down, gf)


def _rope_tables(positions):
    inv_freq = 1.0 / (ROPE_THETA ** (jnp.arange(0, ROPE_DIM, 2, dtype=F32) / ROPE_DIM))
    t = positions.size
    per_row = LANES // ROPE_HALF
    assert t % per_row == 0
    pos = jnp.repeat(positions.astype(F32).reshape(t // per_row, per_row), ROPE_HALF, axis=1)
    angles = lax.optimization_barrier(pos * jnp.tile(inv_freq, per_row))
    cos, sin = lax.optimization_barrier((jnp.cos(angles), jnp.sin(angles)))
    cos, sin = cos.reshape(t, ROPE_HALF), sin.reshape(t, ROPE_HALF)
    zeros = functools.partial(jnp.zeros, dtype=F32)
    c = jnp.concatenate([cos, cos, jnp.ones((t, LANES - ROPE_DIM), F32)], axis=1)
    sa = jnp.concatenate([-sin, zeros((t, LANES - ROPE_HALF))], axis=1)
    sb = jnp.concatenate([zeros((t, ROPE_HALF)), sin, zeros((t, LANES - ROPE_DIM))], axis=1)
    return c, sa, sb


def kernel(x, positions, norm1_g, w_in, lambda_q1, lambda_k1, lambda_q2, lambda_k2, subln_g, w_pool, pool_scale, w_out, norm2_g, w_gate, w_up, conv_w, conv_b, w_down, norm_f_g):
    batch, seq, d = x.shape
    depth = w_in.shape[0]
    pool_width = pool_scale.shape[1]
    attn_width = (w_in.shape[2] - pool_width) // 3
    rope_c, rope_sa, rope_sb = _rope_tables(positions)
    row = lambda v: v.reshape(1, -1)
    h = x.reshape(batch * seq, d)
    for l in range(depth):
        lambda_init = 0.8 - 0.6 * math.exp(-0.3 * l)
        qk, v, u = _inproj(h, row(norm1_g[l]), w_in[l].astype(BF16), rope_c, rope_sa, rope_sb, attn_width)
        pool_w = w_pool[l]
        mlp_tile = _tile(w_gate.shape[2], MLP_TILE)
        attn, w_gate_b, w_up_b, w_down_b, w_out_b, w_pool_b = _attention(
            qk, v, row(lambda_q1[l]), row(lambda_k1[l]), row(lambda_q2[l]), row(lambda_k2[l]), row(subln_g[l]),
            batch, seq, attn_width, lambda_init,
            cast_weights=(w_gate[l], w_up[l], w_down[l], w_out[l], pool_w.reshape(-1, pool_w.shape[-1])),
            cast_tiles=(mlp_tile, mlp_tile, None, None, None))
        pool = _pooling(u, w_pool_b.reshape(pool_w.shape), row(pool_scale[l]), batch, seq)
        h = _outproj(attn, pool, w_out_b, h)
        h = _mlp(h, row(norm2_g[l]), w_gate_b, w_up_b, conv_w[l], row(conv_b[l]), w_down_b, row(norm_f_g), seq,
                 final_norm=(l == depth - 1))
    return h.reshape(batch, seq, d)
```
